```python
import jax, jax.numpy as jnp
from jax import lax
import numpy as np

D_MODEL = 1024
BATCH = 32
SEQ = 256
DEPTH = 4
DEC_BATCH = 4
DEC_SEQ = 4096
PAST_LEN = 512

GRID_W = 64
HEAD_DIM = 64
N_Q_HEADS = 8
N_KV_HEADS = 2
Q_PER_KV = N_Q_HEADS // N_KV_HEADS
ATT_WIDTH = N_Q_HEADS * HEAD_DIM
KV_WIDTH = N_KV_HEADS * HEAD_DIM
WINDOW = 128
BLOCK = 128
ROPE_BASE = 10000.0
N_SGU_GROUPS = 8
SGU_GROUP = 64
SGU_WIDTH = N_SGU_GROUPS * SGU_GROUP
CHUNK = 128
RNN_BLOCKS = 8
RNN_BLOCK = 64
RNN_WIDTH = RNN_BLOCKS * RNN_BLOCK
RNN_CONV = 4
RG_C = 8.0
CONV_WIDTH = 512
CONV_K = 31
N_EXPERTS = 32
TOP_K = 4
D_FF = 1024
SWIGLU_LIMIT = 7.0
SWIGLU_ALPHA = 1.702
N_MOD = 6
EPS = 1e-6
NEG_INF = -1e30
N_AB = (DEPTH + 1) // 2
N_CD = DEPTH // 2
IN_AB = ATT_WIDTH + 2 * KV_WIDTH + 2 * SGU_WIDTH
IN_CD = 2 * RNN_WIDTH + 2 * CONV_WIDTH
MIX_WIDTH = ATT_WIDTH + SGU_WIDTH

kernel_name = 'hybrid_diffusion_prefix_step'

F32 = jnp.float32


def rms_norm(x, g):
    x32 = x.astype(F32)
    y = x32 * lax.rsqrt(jnp.mean(x32 * x32, axis=-1, keepdims=True) + EPS)
    return (y * g.astype(F32)).astype(x.dtype)


def layer_norm(x, g, b=None):
    x32 = x.astype(F32)
    mu = jnp.mean(x32, axis=-1, keepdims=True)
    xc = x32 - mu
    y = xc * lax.rsqrt(jnp.mean(xc * xc, axis=-1, keepdims=True) + EPS) * g.astype(F32)
    if b is not None:
        y = y + b.astype(F32)
    return y.astype(x.dtype)


def adaln(cond, w, b):
    m = jax.nn.silu(cond) @ w + b
    return jnp.split(m[:, None, :], N_MOD, axis=-1)


def sink_softmax(s, sink):
    m = jnp.maximum(jnp.max(s, axis=-1, keepdims=True), sink)
    e = jnp.exp(s - m)
    return e / (jnp.sum(e, axis=-1, keepdims=True) + jnp.exp(sink - m))


def axial_tables(length, dtype):
    rows = length // GRID_W
    grid = jnp.stack(jnp.meshgrid(jnp.arange(rows), jnp.arange(GRID_W), indexing='ij'), -1).reshape(length, 2)
    grid = grid.astype(F32)
    n = HEAD_DIM // 4
    inv = ROPE_BASE ** (-jnp.arange(n, dtype=F32) / n)
    ang_r = grid[:, 0:1] * inv
    ang_c = grid[:, 1:2] * inv
    return tuple(t.astype(dtype) for t in (jnp.cos(ang_r), jnp.sin(ang_r), jnp.cos(ang_c), jnp.sin(ang_c)))


def rope_1d(x, cos, sin):
    n = x.shape[-1] // 2
    shp = (cos.shape[0],) + (1,) * (x.ndim - 3) + (n,)
    c, s = cos.reshape(shp), sin.reshape(shp)
    x1, x2 = x[..., :n], x[..., n:]
    return jnp.concatenate([x1 * c - x2 * s, x2 * c + x1 * s], axis=-1)


def axial_rope(x, tables):
    cr, sr, cc, sc = tables
    h = HEAD_DIM // 2
    return jnp.concatenate([rope_1d(x[..., :h], cr, sr), rope_1d(x[..., h:], cc, sc)], axis=-1)


def context_attention(q, k, v, sink):
    B_, S = q.shape[:2]
    s = jnp.einsum('bqkgd,bskd->bkgqs', q, k, preferred_element_type=F32) * (HEAD_DIM ** -0.5)
    p = sink_softmax(s, sink.astype(F32)[None, :, :, None, None])
    out = jnp.einsum('bkgqs,bskd->bqkgd', p.astype(v.dtype), v)
    return out.reshape(B_, S, ATT_WIDTH)


def latent_attention(q, k, v, k_ctx, v_ctx, sink):
    B_, L = q.shape[:2]
    nb = L // BLOCK
    tables = axial_tables(L, q.dtype)
    q = axial_rope(q, tables)
    k = axial_rope(k, tables)
    qb = q.reshape(B_, nb, BLOCK, N_KV_HEADS, Q_PER_KV, HEAD_DIM)

    def neighbours(t):
        tp = jnp.pad(t, ((0, 0), (BLOCK, BLOCK), (0, 0), (0, 0))).reshape(B_, nb + 2, BLOCK, N_KV_HEADS, HEAD_DIM)
        return jnp.concatenate([tp[:, :-2], tp[:, 1:-1], tp[:, 2:]], axis=2)

    kw, vw = neighbours(k), neighbours(v)
    scale = HEAD_DIM ** -0.5
    s_win = jnp.einsum('bnqkgd,bnskd->bnkgqs', qb, kw, preferred_element_type=F32) * scale
    s_ctx = jnp.einsum('bnqkgd,bpkd->bnkgqp', qb, k_ctx, preferred_element_type=F32) * scale
    qpos = jnp.arange(nb)[:, None, None] * BLOCK + jnp.arange(BLOCK)[None, :, None]
    kpos = (jnp.arange(nb)[:, None, None] - 1) * BLOCK + jnp.arange(3 * BLOCK)[None, None, :]
    valid = (jnp.abs(kpos - qpos) <= WINDOW) & (kpos >= 0) & (kpos < L)
    s_win = jnp.where(valid[None, :, None, None], s_win, NEG_INF)
    p = sink_softmax(jnp.concatenate([s_win, s_ctx], axis=-1), sink.astype(F32)[None, None, :, :, None, None])
    p_win = p[..., :3 * BLOCK].astype(v.dtype)
    p_ctx = p[..., 3 * BLOCK:].astype(v.dtype)
    out = (jnp.einsum('bnkgqs,bnskd->bnqkgd', p_win, vw)
           + jnp.einsum('bnkgqp,bpkd->bnqkgd', p_ctx, v_ctx))
    return out.reshape(B_, L, ATT_WIDTH)


def spatial_gating(u, vg, w_s, b_s, g_sgu):
    B_, L, _ = u.shape
    u = jax.nn.gelu(u)
    vg = layer_norm(jax.nn.gelu(vg), g_sgu)
    vc = vg.reshape(B_, L // CHUNK, CHUNK, N_SGU_GROUPS, SGU_GROUP)
    mixed = jnp.einsum('gpq,bcqgd->bcpgd', w_s, vc) + b_s.T[None, None, :, :, None]
    return u * mixed.reshape(B_, L, SGU_WIDTH)


def mixer_ab(h, w_in, sink, w_s, b_s, g_sgu, w_out, k_ctx, v_ctx):
    B_, L, _ = h.shape
    proj = h @ w_in
    q, k, v, u, vg = jnp.split(proj, [ATT_WIDTH, ATT_WIDTH + KV_WIDTH, ATT_WIDTH + 2 * KV_WIDTH,
                                      ATT_WIDTH + 2 * KV_WIDTH + SGU_WIDTH], axis=-1)
    q = q.reshape(B_, L, N_KV_HEADS, Q_PER_KV, HEAD_DIM)
    k = k.reshape(B_, L, N_KV_HEADS, HEAD_DIM)
    v = v.reshape(B_, L, N_KV_HEADS, HEAD_DIM)
    sink_g = sink.reshape(N_KV_HEADS, Q_PER_KV)
    if k_ctx is None:
        att = context_attention(q, k, v, sink_g)
    else:
        att = latent_attention(q, k, v, k_ctx, v_ctx, sink_g)
    sg = spatial_gating(u, vg, w_s, b_s, g_sgu)
    return jnp.concatenate([att, sg], axis=-1) @ w_out, k, v


def depthwise_conv(x, w, b, pad):
    out = lax.conv_general_dilated(x, w[:, None, :], window_strides=(1,), padding=[pad],
                                   dimension_numbers=('NWC', 'WIO', 'NWC'), feature_group_count=x.shape[-1])
    return out + b


def linear_scan(a, b, h0, reverse):
    idx = -1 if reverse else 0
    b = b.at[:, idx].add(a[:, idx] * h0)

    def combine(l, r):
        return l[0] * r[0], r[0] * l[1] + r[1]

    _, h = lax.associative_scan(combine, (a, b), reverse=reverse, axis=1)
    return h


def rglru_direction(x, w_a, b_a, w_i, b_i, lam, h0, reverse):
    B_, L, _ = x.shape
    xb = x.reshape(B_, L, RNN_BLOCKS, RNN_BLOCK)
    r = jax.nn.sigmoid(jnp.einsum('blnd,nde->blne', xb, w_a).reshape(B_, L, RNN_WIDTH) + b_a)
    i = jax.nn.sigmoid(jnp.einsum('blnd,nde->blne', xb, w_i).reshape(B_, L, RNN_WIDTH) + b_i)
    log_a = -RG_C * r.astype(F32) * jax.nn.softplus(-lam.astype(F32))
    a = jnp.exp(log_a)
    bt = jnp.sqrt(-jnp.expm1(2.0 * log_a)) * (i * x).astype(F32)
    h = linear_scan(a, bt, h0.astype(F32), reverse)
    h_last = h[:, 0] if reverse else h[:, -1]
    return h.astype(x.dtype), h_last.astype(x.dtype)


def mixer_cd(h, w_in, conv_w, conv_b, w_a, b_a, w_i, b_i, lam, dconv_w, dconv_b, ln_g, ln_b, w_out, h0):
    proj = h @ w_in
    gate, xr, glu_a, glu_b = jnp.split(proj, [RNN_WIDTH, 2 * RNN_WIDTH, 2 * RNN_WIDTH + CONV_WIDTH], axis=-1)
    xc = depthwise_conv(xr, conv_w, conv_b, (RNN_CONV // 2, RNN_CONV - 1 - RNN_CONV // 2))
    y_f, h_f = rglru_direction(xc, w_a[0], b_a[0], w_i[0], b_i[0], lam[0], h0[:, 0], False)
    y_b, h_b = rglru_direction(xc, w_a[1], b_a[1], w_i[1], b_i[1], lam[1], h0[:, 1], True)
    out_c = jax.nn.gelu(gate) * (y_f + y_b)
    zd = glu_a * jax.nn.sigmoid(glu_b)
    zd = depthwise_conv(zd, dconv_w, dconv_b, (CONV_K // 2, CONV_K // 2))
    zd = jax.nn.silu(layer_norm(zd, ln_g, ln_b))
    return jnp.concatenate([out_c, zd], axis=-1) @ w_out, jnp.stack([h_f, h_b], axis=1)


def moe(h, w_r, b_r, w_up, b_up, w_down, b_down):
    B_, L, D = h.shape
    t = h.reshape(-1, D)
    logits = (t @ w_r + b_r).astype(F32)
    top_v, top_i = lax.top_k(logits, TOP_K)
    gates = jax.nn.softmax(top_v, axis=-1)
    combine = jnp.sum(jax.nn.one_hot(top_i, N_EXPERTS, dtype=F32) * gates[..., None], axis=1)

    def expert(acc, prm):
        wu, bu, wd, bd, wcol = prm
        gu = t @ wu + bu
        g, u = jnp.split(gu, 2, axis=-1)
        g = jnp.minimum(g, SWIGLU_LIMIT)
        u = jnp.clip(u, -SWIGLU_LIMIT, SWIGLU_LIMIT)
        y = ((u + 1) * (g * jax.nn.sigmoid(SWIGLU_ALPHA * g))) @ wd + bd
        return acc + wcol[:, None] * y.astype(F32), None

    acc, _ = lax.scan(expert, jnp.zeros((t.shape[0], D), F32), (w_up, b_up, w_down, b_down, combine.T))
    return acc.astype(h.dtype).reshape(B_, L, D)


def setup_inputs(seed: int = 0) -> dict:
    key = jax.random.key(seed)
    ks = iter(jax.random.split(key, 40))

    def nrm(shape, scale=1.0):
        return jax.random.normal(next(ks), shape, F32) * scale

    D = D_MODEL
    inp = {}
    inp['x_prompt'] = nrm((BATCH, SEQ, D))
    inp['x_sample'] = nrm((DEC_BATCH, DEC_SEQ, D))
    inp['c'] = nrm((DEC_BATCH, D))
    inp['cache_k'] = nrm((DEC_BATCH, N_AB, PAST_LEN, N_KV_HEADS, HEAD_DIM))
    inp['cache_v'] = nrm((DEC_BATCH, N_AB, PAST_LEN, N_KV_HEADS, HEAD_DIM))
    inp['state_h'] = nrm((DEC_BATCH, N_CD, 2, RNN_WIDTH), 0.5)
    inp['c_ctx'] = nrm((D,))
    inp['w_ada'] = nrm((DEPTH, D, N_MOD * D), 0.5 * D ** -0.5)
    inp['b_ada'] = nrm((DEPTH, N_MOD * D), 0.01)
    inp['g_norm1'] = 1.0 + nrm((DEPTH, D), 0.01)
    inp['g_norm2'] = 1.0 + nrm((DEPTH, D), 0.01)
    inp['g_final'] = 1.0 + nrm((D,), 0.01)
    inp['w_in_ab'] = nrm((N_AB, D, IN_AB), D ** -0.5)
    inp['sink'] = nrm((N_AB, N_Q_HEADS))
    inp['w_spatial'] = nrm((N_AB, N_SGU_GROUPS, CHUNK, CHUNK), CHUNK ** -0.5)
    inp['b_spatial'] = 1.0 + nrm((N_AB, N_SGU_GROUPS, CHUNK), 0.01)
    inp['g_sgu'] = 1.0 + nrm((N_AB, SGU_WIDTH), 0.01)
    inp['w_in_cd'] = nrm((N_CD, D, IN_CD), D ** -0.5)
    inp['conv_c_w'] = nrm((N_CD, RNN_CONV, RNN_WIDTH), RNN_CONV ** -0.5)
    inp['conv_c_b'] = nrm((N_CD, RNN_WIDTH), 0.01)
    inp['w_rg_a'] = nrm((N_CD, 2, RNN_BLOCKS, RNN_BLOCK, RNN_BLOCK), RNN_BLOCK ** -0.5)
    inp['b_rg_a'] = nrm((N_CD, 2, RNN_WIDTH), 0.01)
    inp['w_rg_i'] = nrm((N_CD, 2, RNN_BLOCKS, RNN_BLOCK, RNN_BLOCK), RNN_BLOCK ** -0.5)
    inp['b_rg_i'] = nrm((N_CD, 2, RNN_WIDTH), 0.01)
    a0 = jax.random.uniform(next(ks), (N_CD, 2, RNN_WIDTH), F32, 0.9, 0.999)
    inp['lam'] = jnp.log(a0) - jnp.log1p(-a0)
    inp['conv_d_w'] = nrm((N_CD, CONV_K, CONV_WIDTH), CONV_K ** -0.5)
    inp['conv_d_b'] = nrm((N_CD, CONV_WIDTH), 0.01)
    inp['ln_d_g'] = 1.0 + nrm((N_CD, CONV_WIDTH), 0.01)
    inp['ln_d_b'] = nrm((N_CD, CONV_WIDTH), 0.01)
    inp['w_out'] = nrm((DEPTH, MIX_WIDTH, D), MIX_WIDTH ** -0.5)
    inp['w_router'] = nrm((DEPTH, D, N_EXPERTS), D ** -0.5)
    inp['b_router'] = nrm((DEPTH, N_EXPERTS), 0.01)
    inp['w_up'] = nrm((DEPTH, N_EXPERTS, D, 2 * D_FF), D ** -0.5)
    inp['b_up'] = nrm((DEPTH, N_EXPERTS, 2 * D_FF), 0.01)
    inp['w_down'] = nrm((DEPTH, N_EXPERTS, D_FF, D), D_FF ** -0.5)
    inp['b_down'] = nrm((DEPTH, N_EXPERTS, D), 0.01)
    return inp


def reference(x_prompt, x_sample, c, cache_k, cache_v, state_h, c_ctx, w_ada, b_ada, g_norm1, g_norm2, g_final,
              w_in_ab, sink, w_spatial, b_spatial, g_sgu, w_in_cd, conv_c_w, conv_c_b, w_rg_a, b_rg_a, w_rg_i, b_rg_i,
              lam, conv_d_w, conv_d_b, ln_d_g, ln_d_b, w_out, w_router, b_router, w_up, b_up, w_down, b_down):

    def trunk(x, cond, ctx_k, ctx_v, ctx_h):
        is_context = ctx_k is None
        new_k, new_v, new_h = [], [], []
        for l in range(DEPTH):
            j = l // 2
            sh1, sc1, g1, sh2, sc2, g2 = adaln(cond, w_ada[l], b_ada[l])
            h = rms_norm(x, g_norm1[l]) * (1 + sc1) + sh1
            if l % 2 == 0:
                y, k, v = mixer_ab(h, w_in_ab[j], sink[j], w_spatial[j], b_spatial[j], g_sgu[j], w_out[l],
                                   None if is_context else ctx_k[:, j], None if is_context else ctx_v[:, j])
                if is_context:
                    new_k.append(k)
                    new_v.append(v)
            else:
                h0 = jnp.zeros((x.shape[0], 2, RNN_WIDTH), x.dtype) if is_context else ctx_h[:, j]
                y, h_fin = mixer_cd(h, w_in_cd[j], conv_c_w[j], conv_c_b[j], w_rg_a[j], b_rg_a[j], w_rg_i[j],
                                    b_rg_i[j], lam[j], conv_d_w[j], conv_d_b[j], ln_d_g[j], ln_d_b[j], w_out[l], h0)
                if is_context:
                    new_h.append(h_fin)
            x = x + g1 * y
            h = rms_norm(x, g_norm2[l]) * (1 + sc2) + sh2
            x = x + g2 * moe(h, w_router[l], b_router[l], w_up[l], b_up[l], w_down[l], b_down[l])
        return rms_norm(x, g_final), new_k, new_v, new_h

    y_prompt, ks, vs, hs = trunk(x_prompt, c_ctx[None, :], None, None, None)
    y_sample, _, _, _ = trunk(x_sample, c, cache_k, cache_v, state_h)
    new_k = jnp.stack(ks, axis=1)
    new_v = jnp.stack(vs, axis=1)
    new_h = jnp.stack(hs, axis=1)
    return (y_prompt, y_sample, new_k, new_v, new_h)
```

```python
import functools

import jax
import jax.numpy as jnp
from jax import lax
from jax.experimental import pallas as pl
from jax.experimental.pallas import tpu as pltpu

F32 = jnp.float32
BF16 = jnp.bfloat16

HEAD_DIM = 64
N_Q_HEADS = 8
N_KV_HEADS = 2
Q_PER_KV = N_Q_HEADS // N_KV_HEADS
ATT_WIDTH = N_Q_HEADS * HEAD_DIM
KV_WIDTH = N_KV_HEADS * HEAD_DIM
BLOCK = 128
GRID_W = 64
ROPE_BASE = 10000.0
N_SGU_GROUPS = 8
SGU_GROUP = 64
SGU_WIDTH = N_SGU_GROUPS * SGU_GROUP
CHUNK = 128
RNN_BLOCK = 64
RNN_WIDTH = 512
RNN_CONV = 4
RG_C = 8.0
CONV_WIDTH = 512
CONV_K = 31
N_EXPERTS = 32
TOP_K = 4
SWIGLU_LIMIT = 7.0
SWIGLU_ALPHA = 1.702
N_MOD = 6
EPS = 1e-6
NEG_INF = -1e30

LANES = 128
SUBLANES = 8
VMEM_LIMIT = 52 * 1024 * 1024

ROW_TILE = 512
EXPERT_TILE = 256
COMBINE_TILE = 128
DISPATCH_CHUNK = 64
CONV_HALO = 16


def _params(*sem):
    return pltpu.CompilerParams(dimension_semantics=sem, vmem_limit_bytes=VMEM_LIMIT)


def _gelu(x):
    return jax.nn.gelu(x, approximate=True)


def _rms_mod(x, g, sc, sh):
    h = x * lax.rsqrt(jnp.mean(x * x, axis=-1, keepdims=True) + EPS) * g
    return h * (1.0 + sc) + sh


def _adaln_kernel(c_ref, w_ref, b_ref, o_ref):
    c = c_ref[...]
    s = c * jax.nn.sigmoid(c)
    o_ref[...] = jnp.dot(s.astype(BF16), w_ref[...].astype(BF16), preferred_element_type=F32) + b_ref[...]


def _adaln(cond8, w_ada, b_ada):
    depth, d, n = w_ada.shape
    tn = n // 4
    return pl.pallas_call(
        _adaln_kernel,
        grid=(depth, n // tn),
        in_specs=[pl.BlockSpec((8, d), lambda l, j: (0, 0)),
                  pl.BlockSpec((None, d, tn), lambda l, j: (l, 0, j)),
                  pl.BlockSpec((None, 1, tn), lambda l, j: (l, 0, j))],
        out_specs=pl.BlockSpec((None, 8, tn), lambda l, j: (l, 0, j)),
        out_shape=jax.ShapeDtypeStruct((depth, 8, n), F32),
        compiler_params=_params("arbitrary", "arbitrary"),
        name="adaln",
    )(cond8, w_ada, b_ada.reshape(depth, 1, n))


def _inproj_kernel(x_ref, g_ref, sc_ref, sh_ref, w_ref, *rest, rope):
    h = _rms_mod(x_ref[...], g_ref[...], sc_ref[...], sh_ref[...])
    acc = jnp.dot(h.astype(BF16), w_ref[...], preferred_element_type=F32)
    if not rope:
        (o_ref,) = rest
        o_ref[...] = acc.astype(BF16)
        return
    cos_ref, sin_ref, o_ref, kv_ref = rest
    kv0 = ATT_WIDTH + 2 * SGU_WIDTH
    kv_ref[...] = acc[:, kv0:kv0 + 2 * KV_WIDTH]
    cos = cos_ref[...]
    sin = sin_ref[...]
    lane = lax.broadcasted_iota(jnp.int32, cos.shape, 1)
    first = (lane % 32) < 16
    o_ref[...] = acc.astype(BF16)
    for j in (0, 1, 2, 3, kv0 // LANES):
        seg = acc[:, LANES * j:LANES * (j + 1)]
        partner = jnp.where(first, pltpu.roll(seg, LANES - 16, 1), pltpu.roll(seg, 16, 1))
        o_ref[:, LANES * j:LANES * (j + 1)] = (seg * cos + partner * sin).astype(BF16)


def _inproj(x, g, sc, sh, w, cond_of_tile, tables=None):
    n, d = x.shape
    nout = w.shape[1]
    tm = ROW_TILE
    rope = tables is not None
    in_specs = [pl.BlockSpec((tm, d), lambda i: (i, 0)),
                pl.BlockSpec((1, d), lambda i: (0, 0)),
                pl.BlockSpec((None, 1, d), lambda i: (cond_of_tile(i), 0, 0)),
                pl.BlockSpec((None, 1, d), lambda i: (cond_of_tile(i), 0, 0)),
                pl.BlockSpec((d, nout), lambda i: (0, 0))]
    args = [x, g.reshape(1, d), sc, sh, w]
    out_specs = pl.BlockSpec((tm, nout), lambda i: (i, 0))
    out_shape = jax.ShapeDtypeStruct((n, nout), BF16)
    if rope:
        in_specs += [pl.BlockSpec((tm, LANES), lambda i: (i, 0))] * 2
        args += list(tables)
        out_specs = [out_specs, pl.BlockSpec((tm, 2 * KV_WIDTH), lambda i: (i, 0))]
        out_shape = [out_shape, jax.ShapeDtypeStruct((n, 2 * KV_WIDTH), F32)]
    return pl.pallas_call(
        functools.partial(_inproj_kernel, rope=rope),
        grid=(n // tm,),
        in_specs=in_specs, out_specs=out_specs, out_shape=out_shape,
        compiler_params=_params("arbitrary"),
        name="inproj_rope" if rope else "inproj",
    )(*args)


def _attend(qh, keys, values, masks, sink):
    scale = HEAD_DIM ** -0.5
    scores = []
    for kk, mk in zip(keys, masks):
        s = lax.dot_general(qh, kk, (((1,), (1,)), ((), ())), preferred_element_type=F32) * scale
        if mk is not None:
            s = jnp.where(mk, s, NEG_INF)
        scores.append(s)
    m = sink
    for s in scores:
        m = jnp.maximum(m, jnp.max(s, axis=-1, keepdims=True))
    es = [jnp.exp(s - m) for s in scores]
    den = jnp.exp(sink - m)
    for e in es:
        den = den + jnp.sum(e, axis=-1, keepdims=True)
    inv = 1.0 / den
    out = None
    for e, vv in zip(es, values):
        o = jnp.dot((e * inv).astype(BF16), vv, preferred_element_type=F32)
        out = o if out is None else out + o
    return out


def _ctx_attn_kernel(sink_ref, q_ref, kv_ref, o_ref):
    q = q_ref[...]
    kv = kv_ref[...]
    for h in range(N_Q_HEADS):
        g = h // Q_PER_KV
        qh = q[:, HEAD_DIM * h:HEAD_DIM * (h + 1)]
        kg = kv[:, HEAD_DIM * g:HEAD_DIM * (g + 1)]
        vg = kv[:, KV_WIDTH + HEAD_DIM * g:KV_WIDTH + HEAD_DIM * (g + 1)]
        out = _attend(qh, [kg], [vg], [None], sink_ref[h:h + 1, 0:1])
        o_ref[:, HEAD_DIM * h:HEAD_DIM * (h + 1)] = out.astype(BF16)


def _ctx_attention(proj, sink_b, batch, seq):
    kv_blk = (ATT_WIDTH + 2 * SGU_WIDTH) // (2 * KV_WIDTH)
    return pl.pallas_call(
        _ctx_attn_kernel,
        grid=(batch,),
        in_specs=[pl.BlockSpec((N_Q_HEADS, LANES), lambda b: (0, 0)),
                  pl.BlockSpec((seq, ATT_WIDTH), lambda b: (b, 0)),
                  pl.BlockSpec((seq, 2 * KV_WIDTH), lambda b: (b, kv_blk))],
        out_specs=pl.BlockSpec((seq, ATT_WIDTH), lambda b: (b, 0)),
        out_shape=jax.ShapeDtypeStruct((batch * seq, ATT_WIDTH), BF16),
        compiler_params=_params("arbitrary"),
        name="ctx_attention",
    )(sink_b, proj, proj)


def _lat_attn_kernel(sink_ref, q_ref, kvp_ref, kvc_ref, kvn_ref, ck_ref, cv_ref, o_ref):
    n = pl.program_id(1)
    nb = pl.num_programs(1)
    q = q_ref[...]
    kvw = jnp.concatenate([kvp_ref[...], kvc_ref[...], kvn_ref[...]], axis=0)
    ck = ck_ref[...]
    cv = cv_ref[...]
    qi = lax.broadcasted_iota(jnp.int32, (BLOCK, 3 * BLOCK), 0)
    kj = lax.broadcasted_iota(jnp.int32, (BLOCK, 3 * BLOCK), 1)
    lo = jnp.where(n > 0, 0, BLOCK)
    hi = jnp.where(n < nb - 1, 3 * BLOCK, 2 * BLOCK)
    valid = (kj >= qi) & (kj <= qi + 2 * BLOCK) & (kj >= lo) & (kj < hi)
    for h in range(N_Q_HEADS):
        g = h // Q_PER_KV
        qh = q[:, HEAD_DIM * h:HEAD_DIM * (h + 1)]
        kw = kvw[:, HEAD_DIM * g:HEAD_DIM * (g + 1)]
        vw = kvw[:, KV_WIDTH + HEAD_DIM * g:KV_WIDTH + HEAD_DIM * (g + 1)]
        kc = ck[:, HEAD_DIM * g:HEAD_DIM * (g + 1)]
        vc = cv[:, HEAD_DIM * g:HEAD_DIM * (g + 1)]
        out = _attend(qh, [kw, kc], [vw, vc], [valid, None], sink_ref[h:h + 1, 0:1])
        o_ref[:, HEAD_DIM * h:HEAD_DIM * (h + 1)] = out.astype(BF16)


def _lat_attention(proj, sink_b, ctx_k, ctx_v, row0, batch, length):
    nb = length // BLOCK
    b0 = row0 // BLOCK
    kv_blk = (ATT_WIDTH + 2 * SGU_WIDTH) // (2 * KV_WIDTH)
    past = ctx_k.shape[1]
    kv_spec = lambda f: pl.BlockSpec((BLOCK, 2 * KV_WIDTH), lambda b, n: (b0 + b * nb + f(n), kv_blk))
    return pl.pallas_call(
        _lat_attn_kernel,
        grid=(batch, nb),
        in_specs=[pl.BlockSpec((N_Q_HEADS, LANES), lambda b, n: (0, 0)),
                  pl.BlockSpec((BLOCK, ATT_WIDTH), lambda b, n: (b0 + b * nb + n, 0)),
                  kv_spec(lambda n: jnp.maximum(n - 1, 0)),
                  kv_spec(lambda n: n),
                  kv_spec(lambda n: jnp.minimum(n + 1, nb - 1)),
                  pl.BlockSpec((None, past, KV_WIDTH), lambda b, n: (b, 0, 0)),
                  pl.BlockSpec((None, past, KV_WIDTH), lambda b, n: (b, 0, 0))],
        out_specs=pl.BlockSpec((BLOCK, ATT_WIDTH), lambda b, n: (b * nb + n, 0)),
        out_shape=jax.ShapeDtypeStruct((batch * length, ATT_WIDTH), BF16),
        compiler_params=_params("arbitrary", "arbitrary"),
        name="latent_attention",
    )(sink_b, proj, proj, proj, proj, ctx_k, ctx_v)


def _sgu_kernel(u_ref, v_ref, g_ref, ws_ref, bs_ref, o_ref):
    rows = u_ref.shape[0]
    v = _gelu(v_ref[...].astype(F32))
    mu = jnp.mean(v, axis=-1, keepdims=True)
    vc = v - mu
    v = vc * lax.rsqrt(jnp.mean(vc * vc, axis=-1, keepdims=True) + EPS) * g_ref[...]
    vb = v.astype(BF16)
    for c in range(rows // CHUNK):
        r0 = c * CHUNK
        for g in range(N_SGU_GROUPS):
            c0 = g * SGU_GROUP
            mixed = jnp.dot(ws_ref[g], vb[r0:r0 + CHUNK, c0:c0 + SGU_GROUP], preferred_element_type=F32)
            mixed = mixed + bs_ref[:, g:g + 1]
            u = _gelu(u_ref[r0:r0 + CHUNK, c0:c0 + SGU_GROUP].astype(F32))
            o_ref[r0:r0 + CHUNK, c0:c0 + SGU_GROUP] = (u * mixed).astype(BF16)


def _sgu(proj, g_sgu, w_s, b_s_t):
    n = proj.shape[0]
    tm = 2 * CHUNK
    return pl.pallas_call(
        _sgu_kernel,
        grid=(n // tm,),
        in_specs=[pl.BlockSpec((tm, SGU_WIDTH), lambda i: (i, 1)),
                  pl.BlockSpec((tm, SGU_WIDTH), lambda i: (i, 2)),
                  pl.BlockSpec((1, SGU_WIDTH), lambda i: (0, 0)),
                  pl.BlockSpec((N_SGU_GROUPS, CHUNK, CHUNK), lambda i: (0, 0, 0)),
                  pl.BlockSpec((CHUNK, N_SGU_GROUPS), lambda i: (0, 0))],
        out_specs=pl.BlockSpec((tm, SGU_WIDTH), lambda i: (i, 0)),
        out_shape=jax.ShapeDtypeStruct((n, SGU_WIDTH), BF16),
        compiler_params=_params("arbitrary"),
        name="sgu",
    )(proj, proj, g_sgu.reshape(1, SGU_WIDTH), w_s, b_s_t)


def _scan8(a, b, reverse):
    row = lax.broadcasted_iota(jnp.int32, a.shape, 0)
    for d in (1, 2, 4):
        if reverse:
            keep = row < SUBLANES - d
            shift = SUBLANES - d
        else:
            keep = row >= d
            shift = d
        a_sh = jnp.where(keep, pltpu.roll(a, shift, 0), 1.0)
        b_sh = jnp.where(keep, pltpu.roll(b, shift, 0), 0.0)
        b = b + a * b_sh
        a = a * a_sh
    return a, b


def _rglru_kernel(gate_ref, xr_ref, cw_ref, cb_ref, w4_ref, b4_ref, lam_ref, h0_ref, o_ref, hl_ref,
                  xpad, a_f, b_f, a_b, b_b, y_f, y_b):
    length = xr_ref.shape[0]
    xpad[0:SUBLANES, :] = jnp.zeros((SUBLANES, LANES), F32)
    xpad[SUBLANES:SUBLANES + length, :] = xr_ref[...].astype(F32)
    xpad[SUBLANES + length:, :] = jnp.zeros((SUBLANES, LANES), F32)
    left = RNN_CONV // 2
    xc = cb_ref[...] + cw_ref[0:1, :] * xpad[SUBLANES - left:SUBLANES - left + length, :]
    for j in range(1, RNN_CONV):
        xc = xc + cw_ref[j:j + 1, :] * xpad[SUBLANES - left + j:SUBLANES - left + j + length, :]
    pre = jnp.dot(xc.astype(BF16), w4_ref[...], preferred_element_type=F32) + b4_ref[...]
    lam = lam_ref[...]
    sp = jnp.maximum(-lam, 0.0) + jnp.log(1.0 + jnp.exp(-jnp.abs(lam)))
    for d, (a_ref, b_ref) in enumerate(((a_f, b_f), (a_b, b_b))):
        r = jax.nn.sigmoid(pre[:, 2 * d * LANES:(2 * d + 1) * LANES])
        i = jax.nn.sigmoid(pre[:, (2 * d + 1) * LANES:(2 * d + 2) * LANES])
        log_a = -RG_C * r * sp[d:d + 1, :]
        a = jnp.exp(log_a)
        a_ref[...] = a
        b_ref[...] = jnp.sqrt(1.0 - a * a) * (i * xc)

    groups = length // SUBLANES

    def body(g, carry):
        hf, hb = carry
        rf = pl.multiple_of(g * SUBLANES, SUBLANES)
        rb = pl.multiple_of((groups - 1 - g) * SUBLANES, SUBLANES)
        af, bf = _scan8(a_f[pl.ds(rf, SUBLANES), :], b_f[pl.ds(rf, SUBLANES), :], False)
        ab, bb = _scan8(a_b[pl.ds(rb, SUBLANES), :], b_b[pl.ds(rb, SUBLANES), :], True)
        yf = bf + af * hf
        yb = bb + ab * hb
        y_f[pl.ds(rf, SUBLANES), :] = yf
        y_b[pl.ds(rb, SUBLANES), :] = yb
        return yf[SUBLANES - 1:SUBLANES, :], yb[0:1, :]

    hf, hb = lax.fori_loop(0, groups, body, (h0_ref[0:1, :], h0_ref[1:2, :]))
    hl_ref[0:1, :] = hf
    hl_ref[1:2, :] = hb
    o_ref[...] = (_gelu(gate_ref[...].astype(F32)) * (y_f[...] + y_b[...])).astype(BF16)


def _rglru(proj, row0, batch, length, conv_w, conv_b, w4, b4, lam, h0):
    strips = RNN_WIDTH // LANES
    r0 = row0 // length
    seq = lambda c0: pl.BlockSpec((length, LANES), lambda b, j: (r0 + b, c0 + j))
    strip = lambda rows: pl.BlockSpec((rows, LANES), lambda b, j: (0, j))
    scratch = [pltpu.VMEM((length + 2 * SUBLANES, LANES), F32)] + [pltpu.VMEM((length, LANES), F32)] * 6
    return pl.pallas_call(
        _rglru_kernel,
        grid=(batch, strips),
        in_specs=[seq(0), seq(strips), strip(RNN_CONV), strip(1),
                  pl.BlockSpec((None, LANES, 4 * LANES), lambda b, j: (j, 0, 0)),
                  pl.BlockSpec((None, 1, 4 * LANES), lambda b, j: (j, 0, 0)),
                  strip(2),
                  pl.BlockSpec((None, 2, LANES), lambda b, j: (b, 0, j))],
        out_specs=[pl.BlockSpec((length, LANES), lambda b, j: (b, j)),
                   pl.BlockSpec((None, 2, LANES), lambda b, j: (b, 0, j))],
        out_shape=[jax.ShapeDtypeStruct((batch * length, RNN_WIDTH), BF16),
                   jax.ShapeDtypeStruct((batch, 2, RNN_WIDTH), F32)],
        scratch_shapes=scratch,
        compiler_params=_params("arbitrary", "arbitrary"),
        name="rglru",
    )(proj, proj, conv_w, conv_b.reshape(1, RNN_WIDTH), w4, b4, lam, h0)


def _convmod_kernel(a_ref, b_ref, pa_ref, pb_ref, na_ref, nb_ref, w_ref, cb_ref, g_ref, beta_ref, o_ref, zpad):
    t = pl.program_id(1)
    nt = pl.num_programs(1)
    rows = a_ref.shape[0]

    def glu(x_ref, y_ref):
        return x_ref[...].astype(F32) * jax.nn.sigmoid(y_ref[...].astype(F32))

    zpad[0:CONV_HALO, :] = glu(pa_ref, pb_ref) * (t > 0).astype(F32)
    zpad[CONV_HALO:CONV_HALO + rows, :] = glu(a_ref, b_ref)
    zpad[CONV_HALO + rows:, :] = glu(na_ref, nb_ref) * (t < nt - 1).astype(F32)
    off = CONV_HALO - CONV_K // 2
    acc = cb_ref[...] + w_ref[0:1, :] * zpad[off:off + rows, :]
    for j in range(1, CONV_K):
        acc = acc + w_ref[j:j + 1, :] * zpad[off + j:off + j + rows, :]
    mu = jnp.mean(acc, axis=-1, keepdims=True)
    xc = acc - mu
    y = xc * lax.rsqrt(jnp.mean(xc * xc, axis=-1, keepdims=True) + EPS) * g_ref[...] + beta_ref[...]
    o_ref[...] = (y * jax.nn.sigmoid(y)).astype(BF16)


def _convmod(proj, row0, batch, length, w_pad, cb, ln_g, ln_b):
    tl = min(length, ROW_TILE)
    nt = length // tl
    r0 = row0 // tl
    hb = tl // CONV_HALO
    h0 = row0 // CONV_HALO
    nh = (batch * length + row0) // CONV_HALO
    ca, cbk = 2, 3
    main = lambda c: pl.BlockSpec((tl, CONV_WIDTH), lambda b, t: (r0 + b * nt + t, c))
    prev = lambda c: pl.BlockSpec((CONV_HALO, CONV_WIDTH),
                                  lambda b, t: (jnp.maximum(h0 + (b * nt + t) * hb - 1, 0), c))
    nxt = lambda c: pl.BlockSpec((CONV_HALO, CONV_WIDTH),
                                 lambda b, t: (jnp.minimum(h0 + (b * nt + t + 1) * hb, nh - 1), c))
    vec = pl.BlockSpec((1, CONV_WIDTH), lambda b, t: (0, 0))
    return pl.pallas_call(
        _convmod_kernel,
        grid=(batch, nt),
        in_specs=[main(ca), main(cbk), prev(ca), prev(cbk), nxt(ca), nxt(cbk),
                  pl.BlockSpec((4 * SUBLANES, CONV_WIDTH), lambda b, t: (0, 0)), vec, vec, vec],
        out_specs=pl.BlockSpec((tl, CONV_WIDTH), lambda b, t: (b * nt + t, 0)),
        out_shape=jax.ShapeDtypeStruct((batch * length, CONV_WIDTH), BF16),
        scratch_shapes=[pltpu.VMEM((tl + 2 * CONV_HALO, CONV_WIDTH), F32)],
        compiler_params=_params("arbitrary", "arbitrary"),
        name="convmod",
    )(proj, proj, proj, proj, proj, proj, w_pad, cb.reshape(1, -1), ln_g.reshape(1, -1), ln_b.reshape(1, -1))


def _outproj_kernel(x_ref, ma_ref, mb_ref, wo_ref, g1_ref, gn_ref, sc_ref, sh_ref, wr_ref, br_ref,
                    xo_ref, h_ref, lg_ref):
    half = ma_ref.shape[1]
    y = jnp.dot(ma_ref[...], wo_ref[0:half, :], preferred_element_type=F32)
    y = y + jnp.dot(mb_ref[...], wo_ref[half:, :], preferred_element_type=F32)
    x = x_ref[...] + g1_ref[...] * y
    xo_ref[...] = x
    h = _rms_mod(x, gn_ref[...], sc_ref[...], sh_ref[...])
    h_ref[...] = h
    lg_ref[...] = jnp.dot(h, wr_ref[...], preferred_element_type=F32, precision=lax.Precision.HIGHEST) + br_ref[...]


def _outproj(x, mix_a, mix_b, w_out, g1, gn, sc, sh, w_r, b_r, cond_of_tile):
    n, d = x.shape
    tm = ROW_TILE
    half = mix_a.shape[1]
    row = lambda w: pl.BlockSpec((tm, w), lambda i: (i, 0))
    const = lambda r, c: pl.BlockSpec((r, c), lambda i: (0, 0))
    mod = pl.BlockSpec((None, 1, d), lambda i: (cond_of_tile(i), 0, 0))
    return pl.pallas_call(
        _outproj_kernel,
        grid=(n // tm,),
        in_specs=[row(d), row(half), row(half), const(2 * half, d), mod, const(1, d), mod, mod,
                  const(d, LANES), const(1, LANES)],
        out_specs=[row(d), row(d), row(LANES)],
        out_shape=[jax.ShapeDtypeStruct((n, d), F32), jax.ShapeDtypeStruct((n, d), F32),
                   jax.ShapeDtypeStruct((n, LANES), F32)],
        compiler_params=_params("arbitrary"),
        name="outproj",
    )(x, mix_a, mix_b, w_out, g1, gn.reshape(1, d), sc, sh, w_r, b_r)


def _route_kernel(lg_ref, idx_ref, gate_ref, rank_ref, cnt_ref, carry):
    @pl.when(pl.program_id(0) == 0)
    def _():
        carry[...] = jnp.zeros_like(carry)

    lg = lg_ref[...]
    rows = lg.shape[0]
    lane = lax.broadcasted_iota(jnp.int32, lg.shape, 1).astype(F32)
    work = lg
    vals, idxs = [], []
    for _ in range(TOP_K):
        m = jnp.max(work, axis=-1, keepdims=True)
        ik = jnp.min(jnp.where(work == m, lane, float(LANES)), axis=-1, keepdims=True)
        vals.append(m)
        idxs.append(ik)
        work = jnp.where(lane == ik, -jnp.inf, work)
    es = [jnp.exp(v - vals[0]) for v in vals]
    den = es[0] + es[1] + es[2] + es[3]
    onehot = jnp.zeros(lg.shape, F32)
    for ik in idxs:
        onehot = onehot + (lane == ik).astype(F32)
    ti = lax.broadcasted_iota(jnp.int32, (rows, rows), 0)
    tj = lax.broadcasted_iota(jnp.int32, (rows, rows), 1)
    before = (tj < ti).astype(F32).astype(BF16)
    prior = jnp.dot(before, onehot.astype(BF16), preferred_element_type=F32) + carry[...]
    for k in range(TOP_K):
        idx_ref[:, k:k + 1] = idxs[k].astype(jnp.int32)
        gate_ref[:, k:k + 1] = es[k] / den
        rank = jnp.sum(jnp.where(lane == idxs[k], prior, 0.0), axis=-1, keepdims=True)
        rank_ref[:, k:k + 1] = rank.astype(jnp.int32)
    carry[...] = carry[...] + jnp.sum(onehot, axis=0, keepdims=True)
    cnt_ref[...] = carry[...]


def _route(logits):
    n = logits.shape[0]
    tm = ROW_TILE
    small = lambda dt: jax.ShapeDtypeStruct((n, TOP_K), dt)
    return pl.pallas_call(
        _route_kernel,
        grid=(n // tm,),
        in_specs=[pl.BlockSpec((tm, LANES), lambda i: (i, 0))],
        out_specs=[pl.BlockSpec((tm, TOP_K), lambda i: (i, 0))] * 3 + [pl.BlockSpec((1, LANES), lambda i: (0, 0))],
        out_shape=[small(jnp.int32), small(F32), small(jnp.int32), jax.ShapeDtypeStruct((1, LANES), F32)],
        scratch_shapes=[pltpu.VMEM((1, LANES), F32)],
        compiler_params=_params("arbitrary"),
        name="route",
    )(logits)


def _dispatch_kernel(slot_ref, pad_ref, h_ref, z_ref, xs_ref, sem):
    n = h_ref.shape[0]
    chunks = n // DISPATCH_CHUNK

    def row_copy(tok, k):
        s = slot_ref[tok * TOP_K + k]
        return pltpu.make_async_copy(h_ref.at[pl.ds(tok, 1)], xs_ref.at[pl.ds(s, 1)], sem)

    def start_chunk(c):
        def body(t, carry):
            for k in range(TOP_K):
                row_copy(c * DISPATCH_CHUNK + t, k).start()
            return carry
        lax.fori_loop(0, DISPATCH_CHUNK, body, 0)

    def wait_chunk(c):
        def body(t, carry):
            for k in range(TOP_K):
                row_copy(c * DISPATCH_CHUNK + t, k).wait()
            return carry
        lax.fori_loop(0, DISPATCH_CHUNK, body, 0)

    start_chunk(0)

    def step(c, carry):
        start_chunk(c)
        wait_chunk(c - 1)
        return carry

    lax.fori_loop(1, chunks, step, 0)
    wait_chunk(chunks - 1)

    def pad_copy(r):
        return pltpu.make_async_copy(z_ref.at[pl.ds(0, 1)], xs_ref.at[pl.ds(r, 1)], sem)

    def per_expert(e, carry):
        lo = pad_ref[2 * e]
        hi = pad_ref[2 * e + 1]

        def s_body(r, c2):
            pad_copy(r).start()
            return c2

        def w_body(r, c2):
            pad_copy(r).wait()
            return c2

        lax.fori_loop(lo, hi, s_body, 0)
        lax.fori_loop(lo, hi, w_body, 0)
        return carry

    lax.fori_loop(0, N_EXPERTS, per_expert, 0)


def _dispatch(slots_flat, pad_ranges, h, n_sorted):
    n, d = h.shape
    zero = jnp.zeros((SUBLANES, d), F32)
    grid_spec = pltpu.PrefetchScalarGridSpec(
        num_scalar_prefetch=2, grid=(1,),
        in_specs=[pl.BlockSpec(memory_space=pl.ANY), pl.BlockSpec(memory_space=pl.ANY)],
        out_specs=pl.BlockSpec(memory_space=pl.ANY),
        scratch_shapes=[pltpu.SemaphoreType.DMA(())])
    return pl.pallas_call(
        _dispatch_kernel,
        grid_spec=grid_spec,
        out_shape=jax.ShapeDtypeStruct((n_sorted, d), F32),
        compiler_params=pltpu.CompilerParams(dimension_semantics=("arbitrary",), has_side_effects=True),
        name="dispatch",
    )(slots_flat, pad_ranges, h, zero)


def _expert_kernel(te_ref, nt_ref, xs_ref, wu_ref, bu_ref, wd_ref, bd_ref, ys_ref, wu_bf, wd_bf):
    i = pl.program_id(0)
    active = i < nt_ref[0]
    changed = jnp.logical_or(i == 0, te_ref[i] != te_ref[jnp.maximum(i - 1, 0)])

    @pl.when(jnp.logical_and(active, changed))
    def _():
        wu_bf[...] = wu_ref[...].astype(BF16)
        wd_bf[...] = wd_ref[...].astype(BF16)

    @pl.when(active)
    def _():
        ff = wd_bf.shape[0]
        gu = jnp.dot(xs_ref[...].astype(BF16), wu_bf[...], preferred_element_type=F32) + bu_ref[...]
        g = jnp.minimum(gu[:, :ff], SWIGLU_LIMIT)
        u = jnp.clip(gu[:, ff:], -SWIGLU_LIMIT, SWIGLU_LIMIT)
        act = (u + 1.0) * (g * jax.nn.sigmoid(SWIGLU_ALPHA * g))
        ys_ref[...] = jnp.dot(act.astype(BF16), wd_bf[...], preferred_element_type=F32) + bd_ref[...]

    @pl.when(jnp.logical_not(active))
    def _():
        ys_ref[...] = jnp.zeros_like(ys_ref)


def _experts(tile_expert, n_tiles, xs, w_up, b_up, w_down, b_down):
    p, d = xs.shape
    ff2 = w_up.shape[2]
    ff = w_down.shape[1]
    tm = EXPERT_TILE
    grid_spec = pltpu.PrefetchScalarGridSpec(
        num_scalar_prefetch=2, grid=(p // tm,),
        in_specs=[pl.BlockSpec((tm, d), lambda i, te, nt: (i, 0)),
                  pl.BlockSpec((None, d, ff2), lambda i, te, nt: (te[i], 0, 0)),
                  pl.BlockSpec((None, 1, ff2), lambda i, te, nt: (te[i], 0, 0)),
                  pl.BlockSpec((None, ff, d), lambda i, te, nt: (te[i], 0, 0)),
                  pl.BlockSpec((None, 1, d), lambda i, te, nt: (te[i], 0, 0))],
        out_specs=pl.BlockSpec((tm, d), lambda i, te, nt: (i, 0)),
        scratch_shapes=[pltpu.VMEM((d, ff2), BF16), pltpu.VMEM((ff, d), BF16)])
    return pl.pallas_call(
        _expert_kernel,
        grid_spec=grid_spec,
        out_shape=jax.ShapeDtypeStruct((p, d), F32),
        compiler_params=_params("arbitrary"),
        name="experts",
    )(tile_expert, n_tiles, xs, w_up, b_up.reshape(N_EXPERTS, 1, ff2), w_down, b_down.reshape(N_EXPERTS, 1, d))


def _combine_kernel(slot_ref, x_ref, gate_ref, g2_ref, gf_ref, ys_ref, o_ref, buf, sem, *, final):
    i = pl.program_id(0)
    tc = x_ref.shape[0]

    def row_copy(t, k):
        s = slot_ref[(i * tc + t) * TOP_K + k]
        return pltpu.make_async_copy(ys_ref.at[pl.ds(s, 1)], buf.at[k, pl.ds(t, 1)], sem)

    def s_body(t, carry):
        for k in range(TOP_K):
            row_copy(t, k).start()
        return carry

    def w_body(t, carry):
        for k in range(TOP_K):
            row_copy(t, k).wait()
        return carry

    lax.fori_loop(0, tc, s_body, 0)
    lax.fori_loop(0, tc, w_body, 0)
    gates = gate_ref[...]
    acc = gates[:, 0:1] * buf[0]
    for k in range(1, TOP_K):
        acc = acc + gates[:, k:k + 1] * buf[k]
    x = x_ref[...] + g2_ref[...] * acc
    if final:
        x = x * lax.rsqrt(jnp.mean(x * x, axis=-1, keepdims=True) + EPS) * gf_ref[...]
    o_ref[...] = x


def _combine(slots_flat, x, gates, g2, g_final, ys, cond_of_tile, final):
    n, d = x.shape
    tc = COMBINE_TILE
    per_row_tile = ROW_TILE // tc
    grid_spec = pltpu.PrefetchScalarGridSpec(
        num_scalar_prefetch=1, grid=(n // tc,),
        in_specs=[pl.BlockSpec((tc, d), lambda i, s: (i, 0)),
                  pl.BlockSpec((tc, TOP_K), lambda i, s: (i, 0)),
                  pl.BlockSpec((None, 1, d), lambda i, s: (cond_of_tile(i // per_row_tile), 0, 0)),
                  pl.BlockSpec((1, d), lambda i, s: (0, 0)),
                  pl.BlockSpec(memory_space=pl.ANY)],
        out_specs=pl.BlockSpec((tc, d), lambda i, s: (i, 0)),
        scratch_shapes=[pltpu.VMEM((TOP_K, tc, d), F32), pltpu.SemaphoreType.DMA(())])
    return pl.pallas_call(
        functools.partial(_combine_kernel, final=final),
        grid_spec=grid_spec,
        out_shape=jax.ShapeDtypeStruct((n, d), F32),
        compiler_params=_params("arbitrary"),
        name="combine_final" if final else "combine",
    )(slots_flat, x, gates, g2, g_final.reshape(1, d), ys)


def _moe(x, h, logits, g2, g_final, w_up, b_up, w_down, b_down, cond_of_tile, final):
    n = x.shape[0]
    tm = EXPERT_TILE
    idx, gates, rank, counts = _route(logits)
    counts = counts[0, :N_EXPERTS].astype(jnp.int32)
    padded = ((counts + tm - 1) // tm) * tm
    ends = jnp.cumsum(padded)
    starts = ends - padded
    slots = (starts[idx] + rank).reshape(-1)
    max_tiles = (n * TOP_K) // tm + N_EXPERTS
    n_tiles = ends[-1] // tm
    tile_ids = jnp.minimum(jnp.arange(max_tiles, dtype=jnp.int32), n_tiles - 1)
    tile_expert = jnp.minimum(jnp.searchsorted(ends, tile_ids * tm, side="right"), N_EXPERTS - 1).astype(jnp.int32)
    pad_ranges = jnp.stack([starts + counts, ends], axis=1).reshape(-1).astype(jnp.int32)
    xs = _dispatch(slots, pad_ranges, h, max_tiles * tm)
    ys = _experts(tile_expert, n_tiles.reshape(1).astype(jnp.int32), xs, w_up, b_up, w_down, b_down)
    return _combine(slots, x, gates, g2, g_final, ys, cond_of_tile, final)


def _rope_tables(n_prompt, batch, length):
    rows = jnp.arange(length) // GRID_W
    cols = jnp.arange(length) % GRID_W
    nfreq = HEAD_DIM // 4
    inv = ROPE_BASE ** (-jnp.arange(nfreq, dtype=F32) / nfreq)
    lane = jnp.arange(LANES)
    is_col = (lane % HEAD_DIM) >= HEAD_DIM // 2
    pos = jnp.where(is_col[None, :], cols[:, None], rows[:, None]).astype(F32)
    ang = pos * inv[lane % nfreq][None, :]
    sign = jnp.where((lane % (2 * nfreq)) < nfreq, -1.0, 1.0)[None, :]
    cos = jnp.tile(jnp.cos(ang), (batch, 1))
    sin = jnp.tile(jnp.sin(ang) * sign, (batch, 1))
    cos = jnp.concatenate([jnp.ones((n_prompt, LANES), F32), cos], axis=0)
    sin = jnp.concatenate([jnp.zeros((n_prompt, LANES), F32), sin], axis=0)
    return cos, sin


def _block_diag_pairs(w):
    w = w.reshape(2, 4, 2, RNN_BLOCK, RNN_BLOCK)
    z = jnp.zeros_like(w[:, :, 0])
    top = jnp.concatenate([w[:, :, 0], z], axis=-1)
    bot = jnp.concatenate([z, w[:, :, 1]], axis=-1)
    return jnp.concatenate([top, bot], axis=-2)


def kernel(x_prompt, x_sample, c, cache_k, cache_v, state_h, c_ctx, w_ada, b_ada, g_norm1, g_norm2, g_final, w_in_ab, sink, w_spatial, b_spatial, g_sgu, w_in_cd, conv_c_w, conv_c_b, w_rg_a, b_rg_a, w_rg_i, b_rg_i, lam, conv_d_w, conv_d_b, ln_d_g, ln_d_b, w_out, w_router, b_router, w_up, b_up, w_down, b_down):
    batch, seq, d = x_prompt.shape
    dec_batch, dec_seq, _ = x_sample.shape
    depth = w_ada.shape[0]
    n_p = batch * seq
    n_s = dec_batch * dec_seq
    past = cache_k.shape[2]
    assert n_p % ROW_TILE == 0 and dec_seq % ROW_TILE == 0 and dec_batch + 1 <= 8
    prompt_tiles = n_p // ROW_TILE
    tiles_per_seq = dec_seq // ROW_TILE

    def cond_of_tile(i):
        return jnp.where(i < prompt_tiles, 0, 1 + (i - prompt_tiles) // tiles_per_seq)

    x = jnp.concatenate([x_prompt.reshape(n_p, d), x_sample.reshape(n_s, d)], axis=0)
    cond8 = jnp.zeros((8, d), F32).at[0].set(c_ctx).at[1:1 + dec_batch].set(c)
    mods = _adaln(cond8, w_ada, b_ada)
    mods = mods.reshape(depth, 8, N_MOD, d).transpose(0, 2, 1, 3)[:, :, :, None, :]
    tables = _rope_tables(n_p, dec_batch, dec_seq)

    new_k, new_v, new_h = [], [], []
    for l in range(depth):
        j = l // 2
        sh1, sc1, g1, sh2, sc2, g2 = (mods[l, m] for m in range(N_MOD))
        if l % 2 == 0:
            w = w_in_ab[j]
            q_end, k_end, v_end = ATT_WIDTH, ATT_WIDTH + KV_WIDTH, ATT_WIDTH + 2 * KV_WIDTH
            w = jnp.concatenate([w[:, :q_end], w[:, v_end:], w[:, q_end:v_end]], axis=1).astype(BF16)
            proj, kv = _inproj(x, g_norm1[l], sc1, sh1, w, cond_of_tile, tables)
            new_k.append(kv[:n_p, :KV_WIDTH].reshape(batch, seq, N_KV_HEADS, HEAD_DIM))
            new_v.append(kv[:n_p, KV_WIDTH:].reshape(batch, seq, N_KV_HEADS, HEAD_DIM))
            sink_b = jnp.broadcast_to(sink[j][:, None], (N_Q_HEADS, LANES))
            att_p = _ctx_attention(proj, sink_b, batch, seq)
            ctx_k = cache_k[:, j].reshape(dec_batch, past, KV_WIDTH).astype(BF16)
            ctx_v = cache_v[:, j].reshape(dec_batch, past, KV_WIDTH).astype(BF16)
            att_s = _lat_attention(proj, sink_b, ctx_k, ctx_v, n_p, dec_batch, dec_seq)
            mix_a = jnp.concatenate([att_p, att_s], axis=0)
            mix_b = _sgu(proj, g_sgu[j], w_spatial[j].astype(BF16), b_spatial[j].T)
        else:
            proj = _inproj(x, g_norm1[l], sc1, sh1, w_in_cd[j].astype(BF16), cond_of_tile)
            wa = _block_diag_pairs(w_rg_a[j])
            wi = _block_diag_pairs(w_rg_i[j])
            w4 = jnp.concatenate([wa[0], wi[0], wa[1], wi[1]], axis=-1).astype(BF16)
            strips = lambda v: v.reshape(RNN_WIDTH // LANES, 1, LANES)
            b4 = jnp.concatenate([strips(b_rg_a[j, 0]), strips(b_rg_i[j, 0]),
                                  strips(b_rg_a[j, 1]), strips(b_rg_i[j, 1])], axis=-1)
            h0_p = jnp.zeros((batch, 2, RNN_WIDTH), F32)
            c_p, h_fin = _rglru(proj, 0, batch, seq, conv_c_w[j], conv_c_b[j], w4, b4, lam[j], h0_p)
            c_s, _ = _rglru(proj, n_p, dec_batch, dec_seq, conv_c_w[j], conv_c_b[j], w4, b4, lam[j], state_h[:, j])
            new_h.append(h_fin)
            w_pad = jnp.concatenate([conv_d_w[j], jnp.zeros((4 * SUBLANES - CONV_K, CONV_WIDTH), F32)], axis=0)
            d_p = _convmod(proj, 0, batch, seq, w_pad, conv_d_b[j], ln_d_g[j], ln_d_b[j])
            d_s = _convmod(proj, n_p, dec_batch, dec_seq, w_pad, conv_d_b[j], ln_d_g[j], ln_d_b[j])
            mix_a = jnp.concatenate([c_p, c_s], axis=0)
            mix_b = jnp.concatenate([d_p, d_s], axis=0)
        w_r = jnp.concatenate([w_router[l], jnp.zeros((d, LANES - N_EXPERTS), F32)], axis=1)
        b_r = jnp.concatenate([b_router[l], jnp.full((LANES - N_EXPERTS,), NEG_INF, F32)]).reshape(1, LANES)
        x, h, logits = _outproj(x, mix_a, mix_b, w_out[l].astype(BF16), g1, g_norm2[l], sc2, sh2, w_r, b_r, cond_of_tile)
        x = _moe(x, h, logits, g2, g_final, w_up[l], b_up[l], w_down[l], b_down[l], cond_of_tile, final=(l == depth - 1))

    y_prompt = x[:n_p].reshape(batch, seq, d)
    y_sample = x[n_p:].reshape(dec_batch, dec_seq, d)
    return (y_prompt, y_sample, jnp.stack(new_k, axis=1), jnp.stack(new_v, axis=1), jnp.stack(new_h, axis=1))
```

```python
import functools

import jax
import jax.numpy as jnp
from jax import lax
from jax.experimental import pallas as pl
from jax.experimental.pallas import tpu as pltpu

F32 = jnp.float32
BF16 = jnp.bfloat16

HEAD_DIM = 64
N_Q_HEADS = 8
N_KV_HEADS = 2
Q_PER_KV = N_Q_HEADS // N_KV_HEADS
ATT_WIDTH = N_Q_HEADS * HEAD_DIM
KV_WIDTH = N_KV_HEADS * HEAD_DIM
BLOCK = 128
GRID_W = 64
ROPE_BASE = 10000.0
N_SGU_GROUPS = 8
SGU_GROUP = 64
SGU_WIDTH = N_SGU_GROUPS * SGU_GROUP
CHUNK = 128
RNN_BLOCK = 64
RNN_WIDTH = 512
RNN_CONV = 4
RG_C = 8.0
CONV_WIDTH = 512
CONV_K = 31
N_EXPERTS = 32
TOP_K = 4
SWIGLU_LIMIT = 7.0
SWIGLU_ALPHA = 1.702
N_MOD = 6
EPS = 1e-6
NEG_INF = -1e30

LANES = 128
SUBLANES = 8
VMEM_LIMIT = 52 * 1024 * 1024

ROW_TILE = 512
EXPERT_TILE = 256
COMBINE_TILE = 256
DISPATCH_TILE = 256
COPY_UNROLL = 8
CONV_HALO = 16


def _params(*sem):
    return pltpu.CompilerParams(dimension_semantics=sem, vmem_limit_bytes=VMEM_LIMIT)


def _gelu(x):
    return jax.nn.gelu(x, approximate=True)


def _rms_mod(x, g, sc, sh):
    h = x * lax.rsqrt(jnp.mean(x * x, axis=-1, keepdims=True) + EPS) * g
    return h * (1.0 + sc) + sh


def _adaln_kernel(c_ref, w_ref, b_ref, o_ref):
    c = c_ref[...]
    s = c * jax.nn.sigmoid(c)
    o_ref[...] = jnp.dot(s.astype(BF16), w_ref[...].astype(BF16), preferred_element_type=F32) + b_ref[...]


def _adaln(cond8, w_ada, b_ada):
    depth, d, n = w_ada.shape
    tn = n // 4
    return pl.pallas_call(
        _adaln_kernel,
        grid=(depth, n // tn),
        in_specs=[pl.BlockSpec((8, d), lambda l, j: (0, 0)),
                  pl.BlockSpec((None, d, tn), lambda l, j: (l, 0, j)),
                  pl.BlockSpec((None, 1, tn), lambda l, j: (l, 0, j))],
        out_specs=pl.BlockSpec((None, 8, tn), lambda l, j: (l, 0, j)),
        out_shape=jax.ShapeDtypeStruct((depth, 8, n), F32),
        compiler_params=_params("arbitrary", "arbitrary"),
        name="adaln",
    )(cond8, w_ada, b_ada.reshape(depth, 1, n))


def _inproj_kernel(x_ref, g_ref, sc_ref, sh_ref, w_ref, *rest, rope):
    h = _rms_mod(x_ref[...], g_ref[...], sc_ref[...], sh_ref[...])
    acc = jnp.dot(h.astype(BF16), w_ref[...], preferred_element_type=F32)
    if not rope:
        (o_ref,) = rest
        o_ref[...] = acc.astype(BF16)
        return
    cos_ref, sin_ref, o_ref, kv_ref = rest
    kv0 = ATT_WIDTH + 2 * SGU_WIDTH
    kv_ref[...] = acc[:, kv0:kv0 + 2 * KV_WIDTH]
    cos = cos_ref[...]
    sin = sin_ref[...]
    lane = lax.broadcasted_iota(jnp.int32, cos.shape, 1)
    first = (lane % 32) < 16
    o_ref[...] = acc.astype(BF16)
    for j in (0, 1, 2, 3, kv0 // LANES):
        seg = acc[:, LANES * j:LANES * (j + 1)]
        partner = jnp.where(first, pltpu.roll(seg, LANES - 16, 1), pltpu.roll(seg, 16, 1))
        o_ref[:, LANES * j:LANES * (j + 1)] = (seg * cos + partner * sin).astype(BF16)


def _inproj(x, g, sc, sh, w, cond_of_tile, tables=None):
    n, d = x.shape
    nout = w.shape[1]
    tm = ROW_TILE
    rope = tables is not None
    in_specs = [pl.BlockSpec((tm, d), lambda i: (i, 0)),
                pl.BlockSpec((1, d), lambda i: (0, 0)),
                pl.BlockSpec((None, 1, d), lambda i: (cond_of_tile(i), 0, 0)),
                pl.BlockSpec((None, 1, d), lambda i: (cond_of_tile(i), 0, 0)),
                pl.BlockSpec((d, nout), lambda i: (0, 0))]
    args = [x, g.reshape(1, d), sc, sh, w]
    out_specs = pl.BlockSpec((tm, nout), lambda i: (i, 0))
    out_shape = jax.ShapeDtypeStruct((n, nout), BF16)
    if rope:
        in_specs += [pl.BlockSpec((tm, LANES), lambda i: (i, 0))] * 2
        args += list(tables)
        out_specs = [out_specs, pl.BlockSpec((tm, 2 * KV_WIDTH), lambda i: (i, 0))]
        out_shape = [out_shape, jax.ShapeDtypeStruct((n, 2 * KV_WIDTH), F32)]
    return pl.pallas_call(
        functools.partial(_inproj_kernel, rope=rope),
        grid=(n // tm,),
        in_specs=in_specs, out_specs=out_specs, out_shape=out_shape,
        compiler_params=_params("arbitrary"),
        name="inproj_rope" if rope else "inproj",
    )(*args)


def _attend(qh, keys, values, masks, sink):
    scale = HEAD_DIM ** -0.5
    scores = []
    for kk, mk in zip(keys, masks):
        s = lax.dot_general(qh, kk, (((1,), (1,)), ((), ())), preferred_element_type=F32) * scale
        if mk is not None:
            s = jnp.where(mk, s, NEG_INF)
        scores.append(s)
    m = sink
    for s in scores:
        m = jnp.maximum(m, jnp.max(s, axis=-1, keepdims=True))
    es = [jnp.exp(s - m) for s in scores]
    den = jnp.exp(sink - m)
    for e in es:
        den = den + jnp.sum(e, axis=-1, keepdims=True)
    inv = 1.0 / den
    out = None
    for e, vv in zip(es, values):
        o = jnp.dot((e * inv).astype(BF16), vv, preferred_element_type=F32)
        out = o if out is None else out + o
    return out


def _ctx_attn_kernel(sink_ref, q_ref, kv_ref, o_ref):
    q = q_ref[...]
    kv = kv_ref[...]
    for h in range(N_Q_HEADS):
        g = h // Q_PER_KV
        qh = q[:, HEAD_DIM * h:HEAD_DIM * (h + 1)]
        kg = kv[:, HEAD_DIM * g:HEAD_DIM * (g + 1)]
        vg = kv[:, KV_WIDTH + HEAD_DIM * g:KV_WIDTH + HEAD_DIM * (g + 1)]
        out = _attend(qh, [kg], [vg], [None], sink_ref[h:h + 1, 0:1])
        o_ref[:, HEAD_DIM * h:HEAD_DIM * (h + 1)] = out.astype(BF16)


def _ctx_attention(proj, sink_b, batch, seq):
    kv_blk = (ATT_WIDTH + 2 * SGU_WIDTH) // (2 * KV_WIDTH)
    return pl.pallas_call(
        _ctx_attn_kernel,
        grid=(batch,),
        in_specs=[pl.BlockSpec((N_Q_HEADS, LANES), lambda b: (0, 0)),
                  pl.BlockSpec((seq, ATT_WIDTH), lambda b: (b, 0)),
                  pl.BlockSpec((seq, 2 * KV_WIDTH), lambda b: (b, kv_blk))],
        out_specs=pl.BlockSpec((seq, ATT_WIDTH), lambda b: (b, 0)),
        out_shape=jax.ShapeDtypeStruct((batch * seq, ATT_WIDTH), BF16),
        compiler_params=_params("arbitrary"),
        name="ctx_attention",
    )(sink_b, proj, proj)


def _lat_attn_kernel(sink_ref, q_ref, kvp_ref, kvc_ref, kvn_ref, ck_ref, cv_ref, o_ref):
    n = pl.program_id(1)
    nb = pl.num_programs(1)
    q = q_ref[...]
    kvw = jnp.concatenate([kvp_ref[...], kvc_ref[...], kvn_ref[...]], axis=0)
    ck = ck_ref[...]
    cv = cv_ref[...]
    qi = lax.broadcasted_iota(jnp.int32, (BLOCK, 3 * BLOCK), 0)
    kj = lax.broadcasted_iota(jnp.int32, (BLOCK, 3 * BLOCK), 1)
    lo = jnp.where(n > 0, 0, BLOCK)
    hi = jnp.where(n < nb - 1, 3 * BLOCK, 2 * BLOCK)
    valid = (kj >= qi) & (kj <= qi + 2 * BLOCK) & (kj >= lo) & (kj < hi)
    for h in range(N_Q_HEADS):
        g = h // Q_PER_KV
        qh = q[:, HEAD_DIM * h:HEAD_DIM * (h + 1)]
        kw = kvw[:, HEAD_DIM * g:HEAD_DIM * (g + 1)]
        vw = kvw[:, KV_WIDTH + HEAD_DIM * g:KV_WIDTH + HEAD_DIM * (g + 1)]
        kc = ck[:, HEAD_DIM * g:HEAD_DIM * (g + 1)]
        vc = cv[:, HEAD_DIM * g:HEAD_DIM * (g + 1)]
        out = _attend(qh, [kw, kc], [vw, vc], [valid, None], sink_ref[h:h + 1, 0:1])
        o_ref[:, HEAD_DIM * h:HEAD_DIM * (h + 1)] = out.astype(BF16)


def _lat_attention(proj, sink_b, ctx_k, ctx_v, row0, batch, length):
    nb = length // BLOCK
    b0 = row0 // BLOCK
    kv_blk = (ATT_WIDTH + 2 * SGU_WIDTH) // (2 * KV_WIDTH)
    past = ctx_k.shape[1]
    kv_spec = lambda f: pl.BlockSpec((BLOCK, 2 * KV_WIDTH), lambda b, n: (b0 + b * nb + f(n), kv_blk))
    return pl.pallas_call(
        _lat_attn_kernel,
        grid=(batch, nb),
        in_specs=[pl.BlockSpec((N_Q_HEADS, LANES), lambda b, n: (0, 0)),
                  pl.BlockSpec((BLOCK, ATT_WIDTH), lambda b, n: (b0 + b * nb + n, 0)),
                  kv_spec(lambda n: jnp.maximum(n - 1, 0)),
                  kv_spec(lambda n: n),
                  kv_spec(lambda n: jnp.minimum(n + 1, nb - 1)),
                  pl.BlockSpec((None, past, KV_WIDTH), lambda b, n: (b, 0, 0)),
                  pl.BlockSpec((None, past, KV_WIDTH), lambda b, n: (b, 0, 0))],
        out_specs=pl.BlockSpec((BLOCK, ATT_WIDTH), lambda b, n: (b * nb + n, 0)),
        out_shape=jax.ShapeDtypeStruct((batch * length, ATT_WIDTH), BF16),
        compiler_params=_params("arbitrary", "arbitrary"),
        name="latent_attention",
    )(sink_b, proj, proj, proj, proj, ctx_k, ctx_v)


def _sgu_kernel(u_ref, v_ref, g_ref, ws_ref, bs_ref, o_ref):
    rows = u_ref.shape[0]
    v = _gelu(v_ref[...].astype(F32))
    mu = jnp.mean(v, axis=-1, keepdims=True)
    vc = v - mu
    v = vc * lax.rsqrt(jnp.mean(vc * vc, axis=-1, keepdims=True) + EPS) * g_ref[...]
    vb = v.astype(BF16)
    for c in range(rows // CHUNK):
        r0 = c * CHUNK
        for g in range(N_SGU_GROUPS):
            c0 = g * SGU_GROUP
            mixed = jnp.dot(ws_ref[g], vb[r0:r0 + CHUNK, c0:c0 + SGU_GROUP], preferred_element_type=F32)
            mixed = mixed + bs_ref[:, g:g + 1]
            u = _gelu(u_ref[r0:r0 + CHUNK, c0:c0 + SGU_GROUP].astype(F32))
            o_ref[r0:r0 + CHUNK, c0:c0 + SGU_GROUP] = (u * mixed).astype(BF16)


def _sgu(proj, g_sgu, w_s, b_s_t):
    n = proj.shape[0]
    tm = 2 * CHUNK
    return pl.pallas_call(
        _sgu_kernel,
        grid=(n // tm,),
        in_specs=[pl.BlockSpec((tm, SGU_WIDTH), lambda i: (i, 1)),
                  pl.BlockSpec((tm, SGU_WIDTH), lambda i: (i, 2)),
                  pl.BlockSpec((1, SGU_WIDTH), lambda i: (0, 0)),
                  pl.BlockSpec((N_SGU_GROUPS, CHUNK, CHUNK), lambda i: (0, 0, 0)),
                  pl.BlockSpec((CHUNK, N_SGU_GROUPS), lambda i: (0, 0))],
        out_specs=pl.BlockSpec((tm, SGU_WIDTH), lambda i: (i, 0)),
        out_shape=jax.ShapeDtypeStruct((n, SGU_WIDTH), BF16),
        compiler_params=_params("arbitrary"),
        name="sgu",
    )(proj, proj, g_sgu.reshape(1, SGU_WIDTH), w_s, b_s_t)


def _scan8(a, b, reverse):
    row = lax.broadcasted_iota(jnp.int32, a.shape, 0)
    for d in (1, 2, 4):
        if reverse:
            keep = row < SUBLANES - d
            shift = SUBLANES - d
        else:
            keep = row >= d
            shift = d
        a_sh = jnp.where(keep, pltpu.roll(a, shift, 0), 1.0)
        b_sh = jnp.where(keep, pltpu.roll(b, shift, 0), 0.0)
        b = b + a * b_sh
        a = a * a_sh
    return a, b


def _rglru_kernel(gate_ref, xr_ref, cw_ref, cb_ref, w4_ref, b4_ref, lam_ref, h0_ref, o_ref, hl_ref,
                  xpad, a_f, b_f, a_b, b_b, y_f, y_b):
    length = xr_ref.shape[0]
    xpad[0:SUBLANES, :] = jnp.zeros((SUBLANES, LANES), F32)
    xpad[SUBLANES:SUBLANES + length, :] = xr_ref[...].astype(F32)
    xpad[SUBLANES + length:, :] = jnp.zeros((SUBLANES, LANES), F32)
    left = RNN_CONV // 2
    xc = cb_ref[...] + cw_ref[0:1, :] * xpad[SUBLANES - left:SUBLANES - left + length, :]
    for j in range(1, RNN_CONV):
        xc = xc + cw_ref[j:j + 1, :] * xpad[SUBLANES - left + j:SUBLANES - left + j + length, :]
    pre = jnp.dot(xc.astype(BF16), w4_ref[...], preferred_element_type=F32) + b4_ref[...]
    lam = lam_ref[...]
    sp = jnp.maximum(-lam, 0.0) + jnp.log(1.0 + jnp.exp(-jnp.abs(lam)))
    for d, (a_ref, b_ref) in enumerate(((a_f, b_f), (a_b, b_b))):
        r = jax.nn.sigmoid(pre[:, 2 * d * LANES:(2 * d + 1) * LANES])
        i = jax.nn.sigmoid(pre[:, (2 * d + 1) * LANES:(2 * d + 2) * LANES])
        log_a = -RG_C * r * sp[d:d + 1, :]
        a = jnp.exp(log_a)
        a_ref[...] = a
        b_ref[...] = jnp.sqrt(1.0 - a * a) * (i * xc)

    groups = length // SUBLANES

    def body(g, carry):
        hf, hb = carry
        rf = pl.multiple_of(g * SUBLANES, SUBLANES)
        rb = pl.multiple_of((groups - 1 - g) * SUBLANES, SUBLANES)
        af, bf = _scan8(a_f[pl.ds(rf, SUBLANES), :], b_f[pl.ds(rf, SUBLANES), :], False)
        ab, bb = _scan8(a_b[pl.ds(rb, SUBLANES), :], b_b[pl.ds(rb, SUBLANES), :], True)
        yf = bf + af * hf
        yb = bb + ab * hb
        y_f[pl.ds(rf, SUBLANES), :] = yf
        y_b[pl.ds(rb, SUBLANES), :] = yb
        return yf[SUBLANES - 1:SUBLANES, :], yb[0:1, :]

    hf, hb = lax.fori_loop(0, groups, body, (h0_ref[0:1, :], h0_ref[1:2, :]))
    hl_ref[0:1, :] = hf
    hl_ref[1:2, :] = hb
    o_ref[...] = (_gelu(gate_ref[...].astype(F32)) * (y_f[...] + y_b[...])).astype(BF16)


def _rglru(proj, row0, batch, length, conv_w, conv_b, w4, b4, lam, h0):
    strips = RNN_WIDTH // LANES
    r0 = row0 // length
    seq = lambda c0: pl.BlockSpec((length, LANES), lambda b, j: (r0 + b, c0 + j))
    strip = lambda rows: pl.BlockSpec((rows, LANES), lambda b, j: (0, j))
    scratch = [pltpu.VMEM((length + 2 * SUBLANES, LANES), F32)] + [pltpu.VMEM((length, LANES), F32)] * 6
    return pl.pallas_call(
        _rglru_kernel,
        grid=(batch, strips),
        in_specs=[seq(0), seq(strips), strip(RNN_CONV), strip(1),
                  pl.BlockSpec((None, LANES, 4 * LANES), lambda b, j: (j, 0, 0)),
                  pl.BlockSpec((None, 1, 4 * LANES), lambda b, j: (j, 0, 0)),
                  strip(2),
                  pl.BlockSpec((None, 2, LANES), lambda b, j: (b, 0, j))],
        out_specs=[pl.BlockSpec((length, LANES), lambda b, j: (b, j)),
                   pl.BlockSpec((None, 2, LANES), lambda b, j: (b, 0, j))],
        out_shape=[jax.ShapeDtypeStruct((batch * length, RNN_WIDTH), BF16),
                   jax.ShapeDtypeStruct((batch, 2, RNN_WIDTH), F32)],
        scratch_shapes=scratch,
        compiler_params=_params("arbitrary", "arbitrary"),
        name="rglru",
    )(proj, proj, conv_w, conv_b.reshape(1, RNN_WIDTH), w4, b4, lam, h0)


def _convmod_kernel(a_ref, b_ref, pa_ref, pb_ref, na_ref, nb_ref, w_ref, cb_ref, g_ref, beta_ref, o_ref, zpad):
    t = pl.program_id(1)
    nt = pl.num_programs(1)
    rows = a_ref.shape[0]

    def glu(x_ref, y_ref):
        return x_ref[...].astype(F32) * jax.nn.sigmoid(y_ref[...].astype(F32))

    zpad[0:CONV_HALO, :] = glu(pa_ref, pb_ref) * (t > 0).astype(F32)
    zpad[CONV_HALO:CONV_HALO + rows, :] = glu(a_ref, b_ref)
    zpad[CONV_HALO + rows:, :] = glu(na_ref, nb_ref) * (t < nt - 1).astype(F32)
    off = CONV_HALO - CONV_K // 2
    acc = cb_ref[...] + w_ref[0:1, :] * zpad[off:off + rows, :]
    for j in range(1, CONV_K):
        acc = acc + w_ref[j:j + 1, :] * zpad[off + j:off + j + rows, :]
    mu = jnp.mean(acc, axis=-1, keepdims=True)
    xc = acc - mu
    y = xc * lax.rsqrt(jnp.mean(xc * xc, axis=-1, keepdims=True) + EPS) * g_ref[...] + beta_ref[...]
    o_ref[...] = (y * jax.nn.sigmoid(y)).astype(BF16)


def _convmod(proj, row0, batch, length, w_pad, cb, ln_g, ln_b):
    tl = min(length, ROW_TILE)
    nt = length // tl
    r0 = row0 // tl
    hb = tl // CONV_HALO
    h0 = row0 // CONV_HALO
    nh = (batch * length + row0) // CONV_HALO
    ca, cbk = 2, 3
    main = lambda c: pl.BlockSpec((tl, CONV_WIDTH), lambda b, t: (r0 + b * nt + t, c))
    prev = lambda c: pl.BlockSpec((CONV_HALO, CONV_WIDTH),
                                  lambda b, t: (jnp.maximum(h0 + (b * nt + t) * hb - 1, 0), c))
    nxt = lambda c: pl.BlockSpec((CONV_HALO, CONV_WIDTH),
                                 lambda b, t: (jnp.minimum(h0 + (b * nt + t + 1) * hb, nh - 1), c))
    vec = pl.BlockSpec((1, CONV_WIDTH), lambda b, t: (0, 0))
    return pl.pallas_call(
        _convmod_kernel,
        grid=(batch, nt),
        in_specs=[main(ca), main(cbk), prev(ca), prev(cbk), nxt(ca), nxt(cbk),
                  pl.BlockSpec((4 * SUBLANES, CONV_WIDTH), lambda b, t: (0, 0)), vec, vec, vec],
        out_specs=pl.BlockSpec((tl, CONV_WIDTH), lambda b, t: (b * nt + t, 0)),
        out_shape=jax.ShapeDtypeStruct((batch * length, CONV_WIDTH), BF16),
        scratch_shapes=[pltpu.VMEM((tl + 2 * CONV_HALO, CONV_WIDTH), F32)],
        compiler_params=_params("arbitrary", "arbitrary"),
        name="convmod",
    )(proj, proj, proj, proj, proj, proj, w_pad, cb.reshape(1, -1), ln_g.reshape(1, -1), ln_b.reshape(1, -1))


def _store_row_tiles(ref, val):
    rows, d = val.shape
    seg = d // LANES
    for s in range(seg):
        ref[pl.ds(s, rows, stride=seg), :] = val[:, LANES * s:LANES * (s + 1)]


def _load_row_tiles(ref, rows, seg, base=0):
    return jnp.concatenate([ref[pl.ds(base + s, rows, stride=seg), :] for s in range(seg)], axis=-1)


def _outproj_kernel(x_ref, ma_ref, mb_ref, wo_ref, g1_ref, gn_ref, sc_ref, sh_ref, wr_ref, br_ref,
                    xo_ref, h_ref, lg_ref):
    half = ma_ref.shape[1]
    y = jnp.dot(ma_ref[...], wo_ref[0:half, :], preferred_element_type=F32)
    y = y + jnp.dot(mb_ref[...], wo_ref[half:, :], preferred_element_type=F32)
    x = x_ref[...] + g1_ref[...] * y
    xo_ref[...] = x
    h = _rms_mod(x, gn_ref[...], sc_ref[...], sh_ref[...])
    _store_row_tiles(h_ref, h)
    lg_ref[...] = jnp.dot(h, wr_ref[...], preferred_element_type=F32, precision=lax.Precision.HIGHEST) + br_ref[...]


def _outproj(x, mix_a, mix_b, w_out, g1, gn, sc, sh, w_r, b_r, cond_of_tile):
    n, d = x.shape
    tm = ROW_TILE
    seg = d // LANES
    half = mix_a.shape[1]
    row = lambda w: pl.BlockSpec((tm, w), lambda i: (i, 0))
    const = lambda r, c: pl.BlockSpec((r, c), lambda i: (0, 0))
    mod = pl.BlockSpec((None, 1, d), lambda i: (cond_of_tile(i), 0, 0))
    return pl.pallas_call(
        _outproj_kernel,
        grid=(n // tm,),
        in_specs=[row(d), row(half), row(half), const(2 * half, d), mod, const(1, d), mod, mod,
                  const(d, LANES), const(1, LANES)],
        out_specs=[row(d), pl.BlockSpec((tm * seg, LANES), lambda i: (i, 0)), row(LANES)],
        out_shape=[jax.ShapeDtypeStruct((n, d), F32), jax.ShapeDtypeStruct((n * seg, LANES), F32),
                   jax.ShapeDtypeStruct((n, LANES), F32)],
        compiler_params=_params("arbitrary"),
        name="outproj",
    )(x, mix_a, mix_b, w_out, g1, gn.reshape(1, d), sc, sh, w_r, b_r)


def _route_kernel(lg_ref, idx_ref, gate_ref, rank_ref, cnt_ref, carry):
    @pl.when(pl.program_id(0) == 0)
    def _():
        carry[...] = jnp.zeros_like(carry)

    lg = lg_ref[...]
    rows = lg.shape[0]
    lane = lax.broadcasted_iota(jnp.int32, lg.shape, 1).astype(F32)
    work = lg
    vals, idxs = [], []
    for _ in range(TOP_K):
        m = jnp.max(work, axis=-1, keepdims=True)
        ik = jnp.min(jnp.where(work == m, lane, float(LANES)), axis=-1, keepdims=True)
        vals.append(m)
        idxs.append(ik)
        work = jnp.where(lane == ik, -jnp.inf, work)
    es = [jnp.exp(v - vals[0]) for v in vals]
    den = es[0] + es[1] + es[2] + es[3]
    onehot = jnp.zeros(lg.shape, F32)
    for ik in idxs:
        onehot = onehot + (lane == ik).astype(F32)
    ti = lax.broadcasted_iota(jnp.int32, (rows, rows), 0)
    tj = lax.broadcasted_iota(jnp.int32, (rows, rows), 1)
    before = (tj < ti).astype(F32).astype(BF16)
    prior = jnp.dot(before, onehot.astype(BF16), preferred_element_type=F32) + carry[...]
    for k in range(TOP_K):
        idx_ref[:, k:k + 1] = idxs[k].astype(jnp.int32)
        gate_ref[:, k:k + 1] = es[k] / den
        rank = jnp.sum(jnp.where(lane == idxs[k], prior, 0.0), axis=-1, keepdims=True)
        rank_ref[:, k:k + 1] = rank.astype(jnp.int32)
    carry[...] = carry[...] + jnp.sum(onehot, axis=0, keepdims=True)
    cnt_ref[...] = carry[...]


def _route(logits):
    n = logits.shape[0]
    tm = ROW_TILE
    small = lambda dt: jax.ShapeDtypeStruct((n, TOP_K), dt)
    return pl.pallas_call(
        _route_kernel,
        grid=(n // tm,),
        in_specs=[pl.BlockSpec((tm, LANES), lambda i: (i, 0))],
        out_specs=[pl.BlockSpec((tm, TOP_K), lambda i: (i, 0))] * 3 + [pl.BlockSpec((1, LANES), lambda i: (0, 0))],
        out_shape=[small(jnp.int32), small(F32), small(jnp.int32), jax.ShapeDtypeStruct((1, LANES), F32)],
        scratch_shapes=[pltpu.VMEM((1, LANES), F32)],
        compiler_params=_params("arbitrary"),
        name="route",
    )(logits)


def _row_tile(ref, r):
    return ref.at[pl.ds(pl.multiple_of(r * SUBLANES, SUBLANES), SUBLANES)]


def _dispatch_kernel(slot_ref, pad_ref, h_ref, xs_ref, zero, sem):
    i = pl.program_id(0)
    tc = h_ref.shape[0] // SUBLANES

    def copy(t, k):
        s = slot_ref[(i * tc + t) * TOP_K + k]
        return pltpu.make_async_copy(_row_tile(h_ref, t), _row_tile(xs_ref, s), sem)

    def each(action):
        def body(j, carry):
            for u in range(COPY_UNROLL):
                for k in range(TOP_K):
                    action(copy(j * COPY_UNROLL + u, k))
            return carry
        lax.fori_loop(0, tc // COPY_UNROLL, body, 0)

    each(lambda c: c.start())

    @pl.when(i == 0)
    def _():
        zero[...] = jnp.zeros_like(zero)

        def per_expert(e, carry):
            lo = pad_ref[2 * e]
            hi = pad_ref[2 * e + 1]

            def pad_body(action):
                def body(r, c2):
                    action(pltpu.make_async_copy(zero, _row_tile(xs_ref, r), sem))
                    return c2
                return body

            lax.fori_loop(lo, hi, pad_body(lambda c: c.start()), 0)
            lax.fori_loop(lo, hi, pad_body(lambda c: c.wait()), 0)
            return carry

        lax.fori_loop(0, N_EXPERTS, per_expert, 0)

    each(lambda c: c.wait())


def _dispatch(slots_flat, pad_ranges, h_tiles, n_sorted):
    tc = DISPATCH_TILE
    n = h_tiles.shape[0] // SUBLANES
    grid_spec = pltpu.PrefetchScalarGridSpec(
        num_scalar_prefetch=2, grid=(n // tc,),
        in_specs=[pl.BlockSpec((tc * SUBLANES, LANES), lambda i, s, p: (i, 0))],
        out_specs=pl.BlockSpec(memory_space=pl.ANY),
        scratch_shapes=[pltpu.VMEM((SUBLANES, LANES), F32), pltpu.SemaphoreType.DMA(())])
    return pl.pallas_call(
        _dispatch_kernel,
        grid_spec=grid_spec,
        out_shape=jax.ShapeDtypeStruct((n_sorted * SUBLANES, LANES), F32),
        compiler_params=pltpu.CompilerParams(dimension_semantics=("arbitrary",), has_side_effects=True,
                                             vmem_limit_bytes=VMEM_LIMIT),
        name="dispatch",
    )(slots_flat, pad_ranges, h_tiles)


def _expert_kernel(te_ref, nt_ref, xs_ref, wu_ref, bu_ref, wd_ref, bd_ref, ys_ref, wu_bf, wd_bf):
    i = pl.program_id(0)
    active = i < nt_ref[0]
    changed = jnp.logical_or(i == 0, te_ref[i] != te_ref[jnp.maximum(i - 1, 0)])
    rows = xs_ref.shape[0] // SUBLANES

    @pl.when(jnp.logical_and(active, changed))
    def _():
        wu_bf[...] = wu_ref[...].astype(BF16)
        wd_bf[...] = wd_ref[...].astype(BF16)

    @pl.when(active)
    def _():
        ff = wd_bf.shape[0]
        x = _load_row_tiles(xs_ref, rows, SUBLANES)
        gu = jnp.dot(x.astype(BF16), wu_bf[...], preferred_element_type=F32) + bu_ref[...]
        g = jnp.minimum(gu[:, :ff], SWIGLU_LIMIT)
        u = jnp.clip(gu[:, ff:], -SWIGLU_LIMIT, SWIGLU_LIMIT)
        act = (u + 1.0) * (g * jax.nn.sigmoid(SWIGLU_ALPHA * g))
        y = jnp.dot(act.astype(BF16), wd_bf[...], preferred_element_type=F32) + bd_ref[...]
        _store_row_tiles(ys_ref, y)

    @pl.when(jnp.logical_not(active))
    def _():
        ys_ref[...] = jnp.zeros_like(ys_ref)


def _experts(tile_expert, n_tiles, xs, layer, w_up, b_up, w_down, b_down):
    depth, n_exp, d, ff2 = w_up.shape
    ff = w_down.shape[2]
    tm = EXPERT_TILE
    n_sorted = xs.shape[0] // SUBLANES
    live = lambda i, nt: jnp.minimum(i, nt[0] - 1)
    grid_spec = pltpu.PrefetchScalarGridSpec(
        num_scalar_prefetch=2, grid=(n_sorted // tm,),
        in_specs=[pl.BlockSpec((tm * SUBLANES, LANES), lambda i, te, nt: (live(i, nt), 0)),
                  pl.BlockSpec((None, None, d, ff2), lambda i, te, nt: (layer, te[i], 0, 0)),
                  pl.BlockSpec((None, None, 1, ff2), lambda i, te, nt: (layer, te[i], 0, 0)),
                  pl.BlockSpec((None, None, ff, d), lambda i, te, nt: (layer, te[i], 0, 0)),
                  pl.BlockSpec((None, None, 1, d), lambda i, te, nt: (layer, te[i], 0, 0))],
        out_specs=pl.BlockSpec((tm * SUBLANES, LANES), lambda i, te, nt: (i, 0)),
        scratch_shapes=[pltpu.VMEM((d, ff2), BF16), pltpu.VMEM((ff, d), BF16)])
    return pl.pallas_call(
        _expert_kernel,
        grid_spec=grid_spec,
        out_shape=jax.ShapeDtypeStruct((n_sorted * SUBLANES, LANES), F32),
        compiler_params=_params("arbitrary"),
        name="experts",
    )(tile_expert, n_tiles, xs, w_up, b_up.reshape(depth, n_exp, 1, ff2), w_down, b_down.reshape(depth, n_exp, 1, d))


def _combine_kernel(slot_ref, x_ref, gate_ref, g2_ref, gf_ref, ys_ref, o_ref, buf, sem, *, final):
    i = pl.program_id(0)
    tc = x_ref.shape[0]

    def copy(t, k):
        s = slot_ref[(i * tc + t) * TOP_K + k]
        return pltpu.make_async_copy(_row_tile(ys_ref, s), _row_tile(buf, k * tc + t), sem)

    def each(action):
        def body(j, carry):
            for u in range(COPY_UNROLL):
                for k in range(TOP_K):
                    action(copy(j * COPY_UNROLL + u, k))
            return carry
        lax.fori_loop(0, tc // COPY_UNROLL, body, 0)

    each(lambda c: c.start())
    each(lambda c: c.wait())
    gates = gate_ref[...]
    acc = gates[:, 0:1] * _load_row_tiles(buf, tc, SUBLANES)
    for k in range(1, TOP_K):
        acc = acc + gates[:, k:k + 1] * _load_row_tiles(buf, tc, SUBLANES, base=k * tc * SUBLANES)
    x = x_ref[...] + g2_ref[...] * acc
    if final:
        x = x * lax.rsqrt(jnp.mean(x * x, axis=-1, keepdims=True) + EPS) * gf_ref[...]
    o_ref[...] = x


def _combine(slots_flat, x, gates, g2, g_final, ys, cond_of_tile, final):
    n, d = x.shape
    tc = COMBINE_TILE
    per_row_tile = ROW_TILE // tc
    grid_spec = pltpu.PrefetchScalarGridSpec(
        num_scalar_prefetch=1, grid=(n // tc,),
        in_specs=[pl.BlockSpec((tc, d), lambda i, s: (i, 0)),
                  pl.BlockSpec((tc, TOP_K), lambda i, s: (i, 0)),
                  pl.BlockSpec((None, 1, d), lambda i, s: (cond_of_tile(i // per_row_tile), 0, 0)),
                  pl.BlockSpec((1, d), lambda i, s: (0, 0)),
                  pl.BlockSpec(memory_space=pl.ANY)],
        out_specs=pl.BlockSpec((tc, d), lambda i, s: (i, 0)),
        scratch_shapes=[pltpu.VMEM((TOP_K * tc * SUBLANES, LANES), F32), pltpu.SemaphoreType.DMA(())])
    return pl.pallas_call(
        functools.partial(_combine_kernel, final=final),
        grid_spec=grid_spec,
        out_shape=jax.ShapeDtypeStruct((n, d), F32),
        compiler_params=_params("arbitrary"),
        name="combine_final" if final else "combine",
    )(slots_flat, x, gates, g2, g_final.reshape(1, d), ys)


def _moe(x, h_tiles, logits, g2, g_final, layer, w_up, b_up, w_down, b_down, cond_of_tile, final):
    n = x.shape[0]
    tm = EXPERT_TILE
    idx, gates, rank, counts = _route(logits)
    counts = counts[0, :N_EXPERTS].astype(jnp.int32)
    padded = ((counts + tm - 1) // tm) * tm
    ends = jnp.cumsum(padded)
    starts = ends - padded
    experts = jnp.arange(N_EXPERTS, dtype=jnp.int32)
    start_of = jnp.sum(jnp.where(idx[..., None] == experts, starts, 0), axis=-1)
    slots = (start_of + rank).reshape(-1)
    max_tiles = (n * TOP_K) // tm + N_EXPERTS
    n_tiles = ends[-1] // tm
    tile_rows = jnp.minimum(jnp.arange(max_tiles, dtype=jnp.int32), n_tiles - 1) * tm
    tile_expert = jnp.minimum(jnp.sum(ends[None, :] <= tile_rows[:, None], axis=1), N_EXPERTS - 1).astype(jnp.int32)
    pad_ranges = jnp.stack([starts + counts, ends], axis=1).reshape(-1).astype(jnp.int32)
    xs = _dispatch(slots, pad_ranges, h_tiles, max_tiles * tm)
    ys = _experts(tile_expert, n_tiles.reshape(1).astype(jnp.int32), xs, layer, w_up, b_up, w_down, b_down)
    return _combine(slots, x, gates, g2, g_final, ys, cond_of_tile, final)


def _rope_tables(n_prompt, batch, length):
    rows = jnp.arange(length) // GRID_W
    cols = jnp.arange(length) % GRID_W
    nfreq = HEAD_DIM // 4
    inv = ROPE_BASE ** (-jnp.arange(nfreq, dtype=F32) / nfreq)
    lane = jnp.arange(LANES)
    is_col = (lane % HEAD_DIM) >= HEAD_DIM // 2
    pos = jnp.where(is_col[None, :], cols[:, None], rows[:, None]).astype(F32)
    ang = pos * inv[lane % nfreq][None, :]
    sign = jnp.where((lane % (2 * nfreq)) < nfreq, -1.0, 1.0)[None, :]
    cos = jnp.tile(jnp.cos(ang), (batch, 1))
    sin = jnp.tile(jnp.sin(ang) * sign, (batch, 1))
    cos = jnp.concatenate([jnp.ones((n_prompt, LANES), F32), cos], axis=0)
    sin = jnp.concatenate([jnp.zeros((n_prompt, LANES), F32), sin], axis=0)
    return cos, sin


def _block_diag_pairs(w):
    w = w.reshape(2, 4, 2, RNN_BLOCK, RNN_BLOCK)
    z = jnp.zeros_like(w[:, :, 0])
    top = jnp.concatenate([w[:, :, 0], z], axis=-1)
    bot = jnp.concatenate([z, w[:, :, 1]], axis=-1)
    return jnp.concatenate([top, bot], axis=-2)


def kernel(x_prompt, x_sample, c, cache_k, cache_v, state_h, c_ctx, w_ada, b_ada, g_norm1, g_norm2, g_final, w_in_ab, sink, w_spatial, b_spatial, g_sgu, w_in_cd, conv_c_w, conv_c_b, w_rg_a, b_rg_a, w_rg_i, b_rg_i, lam, conv_d_w, conv_d_b, ln_d_g, ln_d_b, w_out, w_router, b_router, w_up, b_up, w_down, b_down):
    batch, seq, d = x_prompt.shape
    dec_batch, dec_seq, _ = x_sample.shape
    depth = w_ada.shape[0]
    n_p = batch * seq
    n_s = dec_batch * dec_seq
    past = cache_k.shape[2]
    assert n_p % ROW_TILE == 0 and dec_seq % ROW_TILE == 0 and dec_batch + 1 <= 8
    prompt_tiles = n_p // ROW_TILE
    tiles_per_seq = dec_seq // ROW_TILE

    def cond_of_tile(i):
        return jnp.where(i < prompt_tiles, 0, 1 + (i - prompt_tiles) // tiles_per_seq)

    x = jnp.concatenate([x_prompt.reshape(n_p, d), x_sample.reshape(n_s, d)], axis=0)
    cond8 = jnp.zeros((8, d), F32).at[0].set(c_ctx).at[1:1 + dec_batch].set(c)
    mods = _adaln(cond8, w_ada, b_ada)
    mods = mods.reshape(depth, 8, N_MOD, d).transpose(0, 2, 1, 3)[:, :, :, None, :]
    tables = _rope_tables(n_p, dec_batch, dec_seq)

    new_k, new_v, new_h = [], [], []
    for l in range(depth):
        j = l // 2
        sh1, sc1, g1, sh2, sc2, g2 = (mods[l, m] for m in range(N_MOD))
        if l % 2 == 0:
            w = w_in_ab[j]
            q_end, k_end, v_end = ATT_WIDTH, ATT_WIDTH + KV_WIDTH, ATT_WIDTH + 2 * KV_WIDTH
            w = jnp.concatenate([w[:, :q_end], w[:, v_end:], w[:, q_end:v_end]], axis=1).astype(BF16)
            proj, kv = _inproj(x, g_norm1[l], sc1, sh1, w, cond_of_tile, tables)
            new_k.append(kv[:n_p, :KV_WIDTH].reshape(batch, seq, N_KV_HEADS, HEAD_DIM))
            new_v.append(kv[:n_p, KV_WIDTH:].reshape(batch, seq, N_KV_HEADS, HEAD_DIM))
            sink_b = jnp.broadcast_to(sink[j][:, None], (N_Q_HEADS, LANES))
            att_p = _ctx_attention(proj, sink_b, batch, seq)
            ctx_k = cache_k[:, j].reshape(dec_batch, past, KV_WIDTH).astype(BF16)
            ctx_v = cache_v[:, j].reshape(dec_batch, past, KV_WIDTH).astype(BF16)
            att_s = _lat_attention(proj, sink_b, ctx_k, ctx_v, n_p, dec_batch, dec_seq)
            mix_a = jnp.concatenate([att_p, att_s], axis=0)
            mix_b = _sgu(proj, g_sgu[j], w_spatial[j].astype(BF16), b_spatial[j].T)
        else:
            proj = _inproj(x, g_norm1[l], sc1, sh1, w_in_cd[j].astype(BF16), cond_of_tile)
            wa = _block_diag_pairs(w_rg_a[j])
            wi = _block_diag_pairs(w_rg_i[j])
            w4 = jnp.concatenate([wa[0], wi[0], wa[1], wi[1]], axis=-1).astype(BF16)
            strips = lambda v: v.reshape(RNN_WIDTH // LANES, 1, LANES)
            b4 = jnp.concatenate([strips(b_rg_a[j, 0]), strips(b_rg_i[j, 0]),
                                  strips(b_rg_a[j, 1]), strips(b_rg_i[j, 1])], axis=-1)
            h0_p = jnp.zeros((batch, 2, RNN_WIDTH), F32)
            c_p, h_fin = _rglru(proj, 0, batch, seq, conv_c_w[j], conv_c_b[j], w4, b4, lam[j], h0_p)
            c_s, _ = _rglru(proj, n_p, dec_batch, dec_seq, conv_c_w[j], conv_c_b[j], w4, b4, lam[j], state_h[:, j])
            new_h.append(h_fin)
            w_pad = jnp.concatenate([conv_d_w[j], jnp.zeros((4 * SUBLANES - CONV_K, CONV_WIDTH), F32)], axis=0)
            d_p = _convmod(proj, 0, batch, seq, w_pad, conv_d_b[j], ln_d_g[j], ln_d_b[j])
            d_s = _convmod(proj, n_p, dec_batch, dec_seq, w_pad, conv_d_b[j], ln_d_g[j], ln_d_b[j])
            mix_a = jnp.concatenate([c_p, c_s], axis=0)
            mix_b = jnp.concatenate([d_p, d_s], axis=0)
        w_r = jnp.concatenate([w_router[l], jnp.zeros((d, LANES - N_EXPERTS), F32)], axis=1)
        b_r = jnp.concatenate([b_router[l], jnp.full((LANES - N_EXPERTS,), NEG_INF, F32)]).reshape(1, LANES)
        x, h_tiles, logits = _outproj(x, mix_a, mix_b, w_out[l].astype(BF16), g1, g_norm2[l], sc2, sh2, w_r, b_r, cond_of_tile)
        x = _moe(x, h_tiles, logits, g2, g_final, l, w_up, b_up, w_down, b_down, cond_of_tile, final=(l == depth - 1))

    y_prompt = x[:n_p].reshape(batch, seq, d)
    y_sample = x[n_p:].reshape(dec_batch, dec_seq, d)
    return (y_prompt, y_sample, jnp.stack(new_k, axis=1), jnp.stack(new_v, axis=1), jnp.stack(new_h, axis=1))
```

```python
import functools

import jax
import jax.numpy as jnp
from jax import lax
from jax.experimental import pallas as pl
from jax.experimental.pallas import tpu as pltpu

F32 = jnp.float32
BF16 = jnp.bfloat16

HEAD_DIM = 64
N_Q_HEADS = 8
N_KV_HEADS = 2
Q_PER_KV = N_Q_HEADS // N_KV_HEADS
ATT_WIDTH = N_Q_HEADS * HEAD_DIM
KV_WIDTH = N_KV_HEADS * HEAD_DIM
BLOCK = 128
GRID_W = 64
ROPE_BASE = 10000.0
N_SGU_GROUPS = 8
SGU_GROUP = 64
SGU_WIDTH = N_SGU_GROUPS * SGU_GROUP
CHUNK = 128
RNN_BLOCK = 64
RNN_WIDTH = 512
RNN_CONV = 4
RG_C = 8.0
CONV_WIDTH = 512
CONV_K = 31
N_EXPERTS = 32
TOP_K = 4
SWIGLU_LIMIT = 7.0
SWIGLU_ALPHA = 1.702
N_MOD = 6
EPS = 1e-6
NEG_INF = -1e30

LANES = 128
SUBLANES = 8
VMEM_LIMIT = 52 * 1024 * 1024

ROW_TILE = 512
EXPERT_TILE = 256
COMBINE_TILE = 256
COPY_UNROLL = 8
CONV_HALO = 16


def _params(*sem):
    return pltpu.CompilerParams(dimension_semantics=sem, vmem_limit_bytes=VMEM_LIMIT)


def _gelu(x):
    return jax.nn.gelu(x, approximate=True)


def _rms_mod(x, g, sc, sh):
    h = x * lax.rsqrt(jnp.mean(x * x, axis=-1, keepdims=True) + EPS) * g
    return h * (1.0 + sc) + sh


def _adaln_kernel(c_ref, w_ref, b_ref, o_ref):
    c = c_ref[...]
    s = c * jax.nn.sigmoid(c)
    o_ref[...] = jnp.dot(s.astype(BF16), w_ref[...].astype(BF16), preferred_element_type=F32) + b_ref[...]


def _adaln(cond8, w_ada, b_ada):
    depth, d, n = w_ada.shape
    tn = n // 4
    return pl.pallas_call(
        _adaln_kernel,
        grid=(depth, n // tn),
        in_specs=[pl.BlockSpec((8, d), lambda l, j: (0, 0)),
                  pl.BlockSpec((None, d, tn), lambda l, j: (l, 0, j)),
                  pl.BlockSpec((None, 1, tn), lambda l, j: (l, 0, j))],
        out_specs=pl.BlockSpec((None, 8, tn), lambda l, j: (l, 0, j)),
        out_shape=jax.ShapeDtypeStruct((depth, 8, n), F32),
        compiler_params=_params("arbitrary", "arbitrary"),
        name="adaln",
    )(cond8, w_ada, b_ada.reshape(depth, 1, n))


def _inproj_kernel(x_ref, g_ref, sc_ref, sh_ref, w_ref, *rest, rope):
    h = _rms_mod(x_ref[...], g_ref[...], sc_ref[...], sh_ref[...])
    acc = jnp.dot(h.astype(BF16), w_ref[...], preferred_element_type=F32)
    if not rope:
        (o_ref,) = rest
        o_ref[...] = acc.astype(BF16)
        return
    cos_ref, sin_ref, o_ref, kv_ref = rest
    kv0 = ATT_WIDTH + 2 * SGU_WIDTH
    kv_ref[...] = acc[:, kv0:kv0 + 2 * KV_WIDTH]
    cos = cos_ref[...]
    sin = sin_ref[...]
    lane = lax.broadcasted_iota(jnp.int32, cos.shape, 1)
    first = (lane % 32) < 16
    o_ref[...] = acc.astype(BF16)
    for j in (0, 1, 2, 3, kv0 // LANES):
        seg = acc[:, LANES * j:LANES * (j + 1)]
        partner = jnp.where(first, pltpu.roll(seg, LANES - 16, 1), pltpu.roll(seg, 16, 1))
        o_ref[:, LANES * j:LANES * (j + 1)] = (seg * cos + partner * sin).astype(BF16)


def _inproj(x, g, sc, sh, w, cond_of_tile, tables=None):
    n, d = x.shape
    nout = w.shape[1]
    tm = ROW_TILE
    rope = tables is not None
    in_specs = [pl.BlockSpec((tm, d), lambda i: (i, 0)),
                pl.BlockSpec((1, d), lambda i: (0, 0)),
                pl.BlockSpec((None, 1, d), lambda i: (cond_of_tile(i), 0, 0)),
                pl.BlockSpec((None, 1, d), lambda i: (cond_of_tile(i), 0, 0)),
                pl.BlockSpec((d, nout), lambda i: (0, 0))]
    args = [x, g.reshape(1, d), sc, sh, w]
    out_specs = pl.BlockSpec((tm, nout), lambda i: (i, 0))
    out_shape = jax.ShapeDtypeStruct((n, nout), BF16)
    if rope:
        in_specs += [pl.BlockSpec((tm, LANES), lambda i: (i, 0))] * 2
        args += list(tables)
        out_specs = [out_specs, pl.BlockSpec((tm, 2 * KV_WIDTH), lambda i: (i, 0))]
        out_shape = [out_shape, jax.ShapeDtypeStruct((n, 2 * KV_WIDTH), F32)]
    return pl.pallas_call(
        functools.partial(_inproj_kernel, rope=rope),
        grid=(n // tm,),
        in_specs=in_specs, out_specs=out_specs, out_shape=out_shape,
        compiler_params=_params("arbitrary"),
        name="inproj_rope" if rope else "inproj",
    )(*args)


def _attend(qh, keys, values, masks, sink):
    scale = HEAD_DIM ** -0.5
    scores = []
    for kk, mk in zip(keys, masks):
        s = lax.dot_general(qh, kk, (((1,), (1,)), ((), ())), preferred_element_type=F32) * scale
        if mk is not None:
            s = jnp.where(mk, s, NEG_INF)
        scores.append(s)
    m = sink
    for s in scores:
        m = jnp.maximum(m, jnp.max(s, axis=-1, keepdims=True))
    es = [jnp.exp(s - m) for s in scores]
    den = jnp.exp(sink - m)
    for e in es:
        den = den + jnp.sum(e, axis=-1, keepdims=True)
    inv = 1.0 / den
    out = None
    for e, vv in zip(es, values):
        o = jnp.dot((e * inv).astype(BF16), vv, preferred_element_type=F32)
        out = o if out is None else out + o
    return out


def _ctx_attn_kernel(sink_ref, q_ref, kv_ref, o_ref):
    q = q_ref[...]
    kv = kv_ref[...]
    for h in range(N_Q_HEADS):
        g = h // Q_PER_KV
        qh = q[:, HEAD_DIM * h:HEAD_DIM * (h + 1)]
        kg = kv[:, HEAD_DIM * g:HEAD_DIM * (g + 1)]
        vg = kv[:, KV_WIDTH + HEAD_DIM * g:KV_WIDTH + HEAD_DIM * (g + 1)]
        out = _attend(qh, [kg], [vg], [None], sink_ref[h:h + 1, 0:1])
        o_ref[:, HEAD_DIM * h:HEAD_DIM * (h + 1)] = out.astype(BF16)


def _ctx_attention(proj, sink_b, batch, seq):
    kv_blk = (ATT_WIDTH + 2 * SGU_WIDTH) // (2 * KV_WIDTH)
    return pl.pallas_call(
        _ctx_attn_kernel,
        grid=(batch,),
        in_specs=[pl.BlockSpec((N_Q_HEADS, LANES), lambda b: (0, 0)),
                  pl.BlockSpec((seq, ATT_WIDTH), lambda b: (b, 0)),
                  pl.BlockSpec((seq, 2 * KV_WIDTH), lambda b: (b, kv_blk))],
        out_specs=pl.BlockSpec((seq, ATT_WIDTH), lambda b: (b, 0)),
        out_shape=jax.ShapeDtypeStruct((batch * seq, ATT_WIDTH), BF16),
        compiler_params=_params("arbitrary"),
        name="ctx_attention",
    )(sink_b, proj, proj)


def _lat_attn_kernel(sink_ref, q_ref, kvp_ref, kvc_ref, kvn_ref, ck_ref, cv_ref, o_ref):
    n = pl.program_id(1)
    nb = pl.num_programs(1)
    q = q_ref[...]
    kvw = jnp.concatenate([kvp_ref[...], kvc_ref[...], kvn_ref[...]], axis=0)
    ck = ck_ref[...]
    cv = cv_ref[...]
    qi = lax.broadcasted_iota(jnp.int32, (BLOCK, 3 * BLOCK), 0)
    kj = lax.broadcasted_iota(jnp.int32, (BLOCK, 3 * BLOCK), 1)
    lo = jnp.where(n > 0, 0, BLOCK)
    hi = jnp.where(n < nb - 1, 3 * BLOCK, 2 * BLOCK)
    valid = (kj >= qi) & (kj <= qi + 2 * BLOCK) & (kj >= lo) & (kj < hi)
    for h in range(N_Q_HEADS):
        g = h // Q_PER_KV
        qh = q[:, HEAD_DIM * h:HEAD_DIM * (h + 1)]
        kw = kvw[:, HEAD_DIM * g:HEAD_DIM * (g + 1)]
        vw = kvw[:, KV_WIDTH + HEAD_DIM * g:KV_WIDTH + HEAD_DIM * (g + 1)]
        kc = ck[:, HEAD_DIM * g:HEAD_DIM * (g + 1)]
        vc = cv[:, HEAD_DIM * g:HEAD_DIM * (g + 1)]
        out = _attend(qh, [kw, kc], [vw, vc], [valid, None], sink_ref[h:h + 1, 0:1])
        o_ref[:, HEAD_DIM * h:HEAD_DIM * (h + 1)] = out.astype(BF16)


def _lat_attention(proj, sink_b, ctx_k, ctx_v, row0, batch, length):
    nb = length // BLOCK
    b0 = row0 // BLOCK
    kv_blk = (ATT_WIDTH + 2 * SGU_WIDTH) // (2 * KV_WIDTH)
    past = ctx_k.shape[1]
    kv_spec = lambda f: pl.BlockSpec((BLOCK, 2 * KV_WIDTH), lambda b, n: (b0 + b * nb + f(n), kv_blk))
    return pl.pallas_call(
        _lat_attn_kernel,
        grid=(batch, nb),
        in_specs=[pl.BlockSpec((N_Q_HEADS, LANES), lambda b, n: (0, 0)),
                  pl.BlockSpec((BLOCK, ATT_WIDTH), lambda b, n: (b0 + b * nb + n, 0)),
                  kv_spec(lambda n: jnp.maximum(n - 1, 0)),
                  kv_spec(lambda n: n),
                  kv_spec(lambda n: jnp.minimum(n + 1, nb - 1)),
                  pl.BlockSpec((None, past, KV_WIDTH), lambda b, n: (b, 0, 0)),
                  pl.BlockSpec((None, past, KV_WIDTH), lambda b, n: (b, 0, 0))],
        out_specs=pl.BlockSpec((BLOCK, ATT_WIDTH), lambda b, n: (b * nb + n, 0)),
        out_shape=jax.ShapeDtypeStruct((batch * length, ATT_WIDTH), BF16),
        compiler_params=_params("arbitrary", "arbitrary"),
        name="latent_attention",
    )(sink_b, proj, proj, proj, proj, ctx_k, ctx_v)


def _sgu_kernel(u_ref, v_ref, g_ref, ws_ref, bs_ref, o_ref):
    rows = u_ref.shape[0]
    v = _gelu(v_ref[...].astype(F32))
    mu = jnp.mean(v, axis=-1, keepdims=True)
    vc = v - mu
    v = vc * lax.rsqrt(jnp.mean(vc * vc, axis=-1, keepdims=True) + EPS) * g_ref[...]
    vb = v.astype(BF16)
    for c in range(rows // CHUNK):
        r0 = c * CHUNK
        for g in range(N_SGU_GROUPS):
            c0 = g * SGU_GROUP
            mixed = jnp.dot(ws_ref[g], vb[r0:r0 + CHUNK, c0:c0 + SGU_GROUP], preferred_element_type=F32)
            mixed = mixed + bs_ref[:, g:g + 1]
            u = _gelu(u_ref[r0:r0 + CHUNK, c0:c0 + SGU_GROUP].astype(F32))
            o_ref[r0:r0 + CHUNK, c0:c0 + SGU_GROUP] = (u * mixed).astype(BF16)


def _sgu(proj, g_sgu, w_s, b_s_t):
    n = proj.shape[0]
    tm = 2 * CHUNK
    return pl.pallas_call(
        _sgu_kernel,
        grid=(n // tm,),
        in_specs=[pl.BlockSpec((tm, SGU_WIDTH), lambda i: (i, 1)),
                  pl.BlockSpec((tm, SGU_WIDTH), lambda i: (i, 2)),
                  pl.BlockSpec((1, SGU_WIDTH), lambda i: (0, 0)),
                  pl.BlockSpec((N_SGU_GROUPS, CHUNK, CHUNK), lambda i: (0, 0, 0)),
                  pl.BlockSpec((CHUNK, N_SGU_GROUPS), lambda i: (0, 0))],
        out_specs=pl.BlockSpec((tm, SGU_WIDTH), lambda i: (i, 0)),
        out_shape=jax.ShapeDtypeStruct((n, SGU_WIDTH), BF16),
        compiler_params=_params("arbitrary"),
        name="sgu",
    )(proj, proj, g_sgu.reshape(1, SGU_WIDTH), w_s, b_s_t)


def _scan8(a, b, reverse):
    row = lax.broadcasted_iota(jnp.int32, a.shape, 0)
    for d in (1, 2, 4):
        if reverse:
            keep = row < SUBLANES - d
            shift = SUBLANES - d
        else:
            keep = row >= d
            shift = d
        a_sh = jnp.where(keep, pltpu.roll(a, shift, 0), 1.0)
        b_sh = jnp.where(keep, pltpu.roll(b, shift, 0), 0.0)
        b = b + a * b_sh
        a = a * a_sh
    return a, b


def _rglru_kernel(gate_ref, xr_ref, cw_ref, cb_ref, w4_ref, b4_ref, lam_ref, h0_ref, o_ref, hl_ref,
                  xpad, a_f, b_f, a_b, b_b, y_f, y_b):
    length = xr_ref.shape[0]
    xpad[0:SUBLANES, :] = jnp.zeros((SUBLANES, LANES), F32)
    xpad[SUBLANES:SUBLANES + length, :] = xr_ref[...].astype(F32)
    xpad[SUBLANES + length:, :] = jnp.zeros((SUBLANES, LANES), F32)
    left = RNN_CONV // 2
    xc = cb_ref[...] + cw_ref[0:1, :] * xpad[SUBLANES - left:SUBLANES - left + length, :]
    for j in range(1, RNN_CONV):
        xc = xc + cw_ref[j:j + 1, :] * xpad[SUBLANES - left + j:SUBLANES - left + j + length, :]
    pre = jnp.dot(xc.astype(BF16), w4_ref[...], preferred_element_type=F32) + b4_ref[...]
    lam = lam_ref[...]
    sp = jnp.maximum(-lam, 0.0) + jnp.log(1.0 + jnp.exp(-jnp.abs(lam)))
    for d, (a_ref, b_ref) in enumerate(((a_f, b_f), (a_b, b_b))):
        r = jax.nn.sigmoid(pre[:, 2 * d * LANES:(2 * d + 1) * LANES])
        i = jax.nn.sigmoid(pre[:, (2 * d + 1) * LANES:(2 * d + 2) * LANES])
        log_a = -RG_C * r * sp[d:d + 1, :]
        a = jnp.exp(log_a)
        a_ref[...] = a
        b_ref[...] = jnp.sqrt(1.0 - a * a) * (i * xc)

    groups = length // SUBLANES

    def body(g, carry):
        hf, hb = carry
        rf = pl.multiple_of(g * SUBLANES, SUBLANES)
        rb = pl.multiple_of((groups - 1 - g) * SUBLANES, SUBLANES)
        af, bf = _scan8(a_f[pl.ds(rf, SUBLANES), :], b_f[pl.ds(rf, SUBLANES), :], False)
        ab, bb = _scan8(a_b[pl.ds(rb, SUBLANES), :], b_b[pl.ds(rb, SUBLANES), :], True)
        yf = bf + af * hf
        yb = bb + ab * hb
        y_f[pl.ds(rf, SUBLANES), :] = yf
        y_b[pl.ds(rb, SUBLANES), :] = yb
        return yf[SUBLANES - 1:SUBLANES, :], yb[0:1, :]

    hf, hb = lax.fori_loop(0, groups, body, (h0_ref[0:1, :], h0_ref[1:2, :]))
    hl_ref[0:1, :] = hf
    hl_ref[1:2, :] = hb
    o_ref[...] = (_gelu(gate_ref[...].astype(F32)) * (y_f[...] + y_b[...])).astype(BF16)


def _rglru(proj, row0, batch, length, conv_w, conv_b, w4, b4, lam, h0):
    strips = RNN_WIDTH // LANES
    r0 = row0 // length
    seq = lambda c0: pl.BlockSpec((length, LANES), lambda b, j: (r0 + b, c0 + j))
    strip = lambda rows: pl.BlockSpec((rows, LANES), lambda b, j: (0, j))
    scratch = [pltpu.VMEM((length + 2 * SUBLANES, LANES), F32)] + [pltpu.VMEM((length, LANES), F32)] * 6
    return pl.pallas_call(
        _rglru_kernel,
        grid=(batch, strips),
        in_specs=[seq(0), seq(strips), strip(RNN_CONV), strip(1),
                  pl.BlockSpec((None, LANES, 4 * LANES), lambda b, j: (j, 0, 0)),
                  pl.BlockSpec((None, 1, 4 * LANES), lambda b, j: (j, 0, 0)),
                  strip(2),
                  pl.BlockSpec((None, 2, LANES), lambda b, j: (b, 0, j))],
        out_specs=[pl.BlockSpec((length, LANES), lambda b, j: (b, j)),
                   pl.BlockSpec((None, 2, LANES), lambda b, j: (b, 0, j))],
        out_shape=[jax.ShapeDtypeStruct((batch * length, RNN_WIDTH), BF16),
                   jax.ShapeDtypeStruct((batch, 2, RNN_WIDTH), F32)],
        scratch_shapes=scratch,
        compiler_params=_params("arbitrary", "arbitrary"),
        name="rglru",
    )(proj, proj, conv_w, conv_b.reshape(1, RNN_WIDTH), w4, b4, lam, h0)


def _convmod_kernel(a_ref, b_ref, pa_ref, pb_ref, na_ref, nb_ref, w_ref, cb_ref, g_ref, beta_ref, o_ref, zpad):
    t = pl.program_id(1)
    nt = pl.num_programs(1)
    rows = a_ref.shape[0]

    def glu(x_ref, y_ref):
        return x_ref[...].astype(F32) * jax.nn.sigmoid(y_ref[...].astype(F32))

    zpad[0:CONV_HALO, :] = glu(pa_ref, pb_ref) * (t > 0).astype(F32)
    zpad[CONV_HALO:CONV_HALO + rows, :] = glu(a_ref, b_ref)
    zpad[CONV_HALO + rows:, :] = glu(na_ref, nb_ref) * (t < nt - 1).astype(F32)
    off = CONV_HALO - CONV_K // 2
    acc = cb_ref[...] + w_ref[0:1, :] * zpad[off:off + rows, :]
    for j in range(1, CONV_K):
        acc = acc + w_ref[j:j + 1, :] * zpad[off + j:off + j + rows, :]
    mu = jnp.mean(acc, axis=-1, keepdims=True)
    xc = acc - mu
    y = xc * lax.rsqrt(jnp.mean(xc * xc, axis=-1, keepdims=True) + EPS) * g_ref[...] + beta_ref[...]
    o_ref[...] = (y * jax.nn.sigmoid(y)).astype(BF16)


def _convmod(proj, row0, batch, length, w_pad, cb, ln_g, ln_b):
    tl = min(length, ROW_TILE)
    nt = length // tl
    r0 = row0 // tl
    hb = tl // CONV_HALO
    h0 = row0 // CONV_HALO
    nh = (batch * length + row0) // CONV_HALO
    ca, cbk = 2, 3
    main = lambda c: pl.BlockSpec((tl, CONV_WIDTH), lambda b, t: (r0 + b * nt + t, c))
    prev = lambda c: pl.BlockSpec((CONV_HALO, CONV_WIDTH),
                                  lambda b, t: (jnp.maximum(h0 + (b * nt + t) * hb - 1, 0), c))
    nxt = lambda c: pl.BlockSpec((CONV_HALO, CONV_WIDTH),
                                 lambda b, t: (jnp.minimum(h0 + (b * nt + t + 1) * hb, nh - 1), c))
    vec = pl.BlockSpec((1, CONV_WIDTH), lambda b, t: (0, 0))
    return pl.pallas_call(
        _convmod_kernel,
        grid=(batch, nt),
        in_specs=[main(ca), main(cbk), prev(ca), prev(cbk), nxt(ca), nxt(cbk),
                  pl.BlockSpec((4 * SUBLANES, CONV_WIDTH), lambda b, t: (0, 0)), vec, vec, vec],
        out_specs=pl.BlockSpec((tl, CONV_WIDTH), lambda b, t: (b * nt + t, 0)),
        out_shape=jax.ShapeDtypeStruct((batch * length, CONV_WIDTH), BF16),
        scratch_shapes=[pltpu.VMEM((tl + 2 * CONV_HALO, CONV_WIDTH), F32)],
        compiler_params=_params("arbitrary", "arbitrary"),
        name="convmod",
    )(proj, proj, proj, proj, proj, proj, w_pad, cb.reshape(1, -1), ln_g.reshape(1, -1), ln_b.reshape(1, -1))


def _store_row_tiles(ref, val):
    rows, d = val.shape
    seg = d // LANES
    for s in range(seg):
        ref[pl.ds(s, rows, stride=seg), :] = val[:, LANES * s:LANES * (s + 1)]


def _load_row_tiles(ref, rows, seg, base=0):
    return jnp.concatenate([ref[pl.ds(base + s, rows, stride=seg), :] for s in range(seg)], axis=-1)


def _outproj_kernel(x_ref, ma_ref, mb_ref, wo_ref, g1_ref, gn_ref, sc_ref, sh_ref, wr_ref, br_ref,
                    xo_ref, h_ref, lg_ref):
    half = ma_ref.shape[1]
    y = jnp.dot(ma_ref[...], wo_ref[0:half, :], preferred_element_type=F32)
    y = y + jnp.dot(mb_ref[...], wo_ref[half:, :], preferred_element_type=F32)
    x = x_ref[...] + g1_ref[...] * y
    xo_ref[...] = x
    h = _rms_mod(x, gn_ref[...], sc_ref[...], sh_ref[...])
    _store_row_tiles(h_ref, h)
    h_hi = h.astype(BF16)
    h_lo = (h - h_hi.astype(F32)).astype(BF16)
    r = jnp.dot(h_hi, wr_ref[...], preferred_element_type=F32)
    r = r[:, :LANES] + r[:, LANES:] + jnp.dot(h_lo, wr_ref[:, :LANES], preferred_element_type=F32)
    lg_ref[...] = r + br_ref[...]


def _outproj(x, mix_a, mix_b, w_out, g1, gn, sc, sh, w_r, b_r, cond_of_tile):
    n, d = x.shape
    tm = ROW_TILE
    seg = d // LANES
    half = mix_a.shape[1]
    row = lambda w: pl.BlockSpec((tm, w), lambda i: (i, 0))
    const = lambda r, c: pl.BlockSpec((r, c), lambda i: (0, 0))
    mod = pl.BlockSpec((None, 1, d), lambda i: (cond_of_tile(i), 0, 0))
    return pl.pallas_call(
        _outproj_kernel,
        grid=(n // tm,),
        in_specs=[row(d), row(half), row(half), const(2 * half, d), mod, const(1, d), mod, mod,
                  const(d, 2 * LANES), const(1, LANES)],
        out_specs=[row(d), pl.BlockSpec((tm * seg, LANES), lambda i: (i, 0)), row(LANES)],
        out_shape=[jax.ShapeDtypeStruct((n, d), F32), jax.ShapeDtypeStruct((n * seg, LANES), F32),
                   jax.ShapeDtypeStruct((n, LANES), F32)],
        compiler_params=_params("arbitrary"),
        name="outproj",
    )(x, mix_a, mix_b, w_out, g1, gn.reshape(1, d), sc, sh, w_r, b_r)


def _route_kernel(lg_ref, idx_ref, gate_ref, cnt_ref, carry):
    @pl.when(pl.program_id(0) == 0)
    def _():
        carry[...] = jnp.zeros_like(carry)

    lg = lg_ref[...]
    lane = lax.broadcasted_iota(jnp.int32, lg.shape, 1).astype(F32)
    work = lg
    vals, idxs = [], []
    for _ in range(TOP_K):
        m = jnp.max(work, axis=-1, keepdims=True)
        ik = jnp.min(jnp.where(work == m, lane, float(LANES)), axis=-1, keepdims=True)
        vals.append(m)
        idxs.append(ik)
        work = jnp.where(lane == ik, -jnp.inf, work)
    es = [jnp.exp(v - vals[0]) for v in vals]
    den = es[0] + es[1] + es[2] + es[3]
    onehot = jnp.zeros(lg.shape, F32)
    for k in range(TOP_K):
        idx_ref[:, k:k + 1] = idxs[k].astype(jnp.int32)
        gate_ref[:, k:k + 1] = es[k] / den
        onehot = onehot + (lane == idxs[k]).astype(F32)
    carry[...] = carry[...] + jnp.sum(onehot, axis=0, keepdims=True)
    cnt_ref[...] = carry[...]


def _route(logits):
    n = logits.shape[0]
    tm = ROW_TILE
    small = lambda dt: jax.ShapeDtypeStruct((n, TOP_K), dt)
    return pl.pallas_call(
        _route_kernel,
        grid=(n // tm,),
        in_specs=[pl.BlockSpec((tm, LANES), lambda i: (i, 0))],
        out_specs=[pl.BlockSpec((tm, TOP_K), lambda i: (i, 0))] * 2 + [pl.BlockSpec((1, LANES), lambda i: (0, 0))],
        out_shape=[small(jnp.int32), small(F32), jax.ShapeDtypeStruct((1, LANES), F32)],
        scratch_shapes=[pltpu.VMEM((1, LANES), F32)],
        compiler_params=_params("arbitrary"),
        name="route",
    )(logits)


def _row_tile(ref, r):
    return ref.at[pl.ds(pl.multiple_of(r * SUBLANES, SUBLANES), SUBLANES)]


def _expert_kernel(a_ref, te_ref, nt_ref, h_ref, wu_ref, bu_ref, wd_ref, bd_ref, ys_ref,
                   xbuf0, xbuf1, ybuf0, ybuf1, wu_bf, wd_bf, sem_g, sem_s):
    j = pl.program_id(0)
    tm = EXPERT_TILE
    n_tok = h_ref.shape[0] // SUBLANES
    n_tiles = nt_ref[0]
    xbufs, ybufs = (xbuf0, xbuf1), (ybuf0, ybuf1)

    def gather(tile, s, i):
        a = a_ref[(tile + 1) * tm + i]
        tok = jnp.minimum(a // TOP_K, n_tok - 1)
        return pltpu.make_async_copy(_row_tile(h_ref, tok), xbufs[s].at[pl.ds(i * SUBLANES, SUBLANES)], sem_g.at[s])

    def scatter(tile, s, i):
        a = a_ref[(tile + 1) * tm + i]
        return pltpu.make_async_copy(ybufs[s].at[pl.ds(i * SUBLANES, SUBLANES)], _row_tile(ys_ref, a), sem_s.at[s])

    def in_loop(make, act):
        def body(g, carry):
            for u in range(COPY_UNROLL):
                act(make(g * COPY_UNROLL + u))
            return carry
        lax.fori_loop(0, tm // COPY_UNROLL, body, 0)

    @pl.when(j == 0)
    def _():
        ybuf1[...] = jnp.zeros_like(ybuf1)
        in_loop(lambda i: gather(0, 0, i), lambda c: c.start())

    changed = jnp.logical_or(j == 0, te_ref[j] != te_ref[jnp.maximum(j - 1, 0)])

    @pl.when(jnp.logical_and(j <= n_tiles, changed))
    def _():
        wu_bf[...] = wu_ref[...].astype(BF16)
        wd_bf[...] = wd_ref[...].astype(BF16)

    for s in range(2):
        @pl.when(jnp.logical_and(j <= n_tiles + 1, j % 2 == s))
        def _():
            in_loop(lambda i: gather(j, s, i), lambda c: c.wait())

            @pl.when(j >= 1)
            def _():
                in_loop(lambda i: scatter(j - 2, s, i), lambda c: c.wait())

        @pl.when(jnp.logical_and(j <= n_tiles, j % 2 == s))
        def _():
            for i in range(tm):
                gather(j + 1, 1 - s, i).start()
            for i in range(tm):
                scatter(j - 1, 1 - s, i).start()
            ff = wd_bf.shape[0]
            x = _load_row_tiles(xbufs[s], tm, SUBLANES)
            gu = jnp.dot(x.astype(BF16), wu_bf[...], preferred_element_type=F32) + bu_ref[...]
            g = jnp.minimum(gu[:, :ff], SWIGLU_LIMIT)
            u = jnp.clip(gu[:, ff:], -SWIGLU_LIMIT, SWIGLU_LIMIT)
            act = (u + 1.0) * (g * jax.nn.sigmoid(SWIGLU_ALPHA * g))
            y = jnp.dot(act.astype(BF16), wd_bf[...], preferred_element_type=F32) + bd_ref[...]
            _store_row_tiles(ybufs[s], y)


def _experts(a_ext, tile_expert, n_tiles, h_tiles, layer, w_up, b_up, w_down, b_down):
    depth, n_exp, d, ff2 = w_up.shape
    ff = w_down.shape[2]
    tm = EXPERT_TILE
    n_tok = h_tiles.shape[0] // SUBLANES
    pick = lambda j, a, te, nt: (layer, te[j], 0, 0)
    tile_buf = pltpu.VMEM((tm * SUBLANES, LANES), F32)
    grid_spec = pltpu.PrefetchScalarGridSpec(
        num_scalar_prefetch=3, grid=(tile_expert.shape[0],),
        in_specs=[pl.BlockSpec(memory_space=pl.ANY),
                  pl.BlockSpec((None, None, d, ff2), pick),
                  pl.BlockSpec((None, None, 1, ff2), pick),
                  pl.BlockSpec((None, None, ff, d), pick),
                  pl.BlockSpec((None, None, 1, d), pick)],
        out_specs=pl.BlockSpec(memory_space=pl.ANY),
        scratch_shapes=[tile_buf, tile_buf, tile_buf, tile_buf,
                        pltpu.VMEM((d, ff2), BF16), pltpu.VMEM((ff, d), BF16),
                        pltpu.SemaphoreType.DMA((2,)), pltpu.SemaphoreType.DMA((2,))])
    return pl.pallas_call(
        _expert_kernel,
        grid_spec=grid_spec,
        out_shape=jax.ShapeDtypeStruct(((TOP_K * n_tok + tm) * SUBLANES, LANES), F32),
        compiler_params=pltpu.CompilerParams(dimension_semantics=("arbitrary",), has_side_effects=True,
                                             vmem_limit_bytes=VMEM_LIMIT),
        name="experts",
    )(a_ext, tile_expert, n_tiles, h_tiles,
      w_up, b_up.reshape(depth, n_exp, 1, ff2), w_down, b_down.reshape(depth, n_exp, 1, d))


def _combine_kernel(x_ref, gate_ref, g2_ref, gf_ref, y_ref, o_ref, *, final):
    tc = x_ref.shape[0]
    gates = gate_ref[...]
    acc = None
    for k in range(TOP_K):
        yk = jnp.concatenate([y_ref[pl.ds(k * SUBLANES + c, tc, stride=TOP_K * SUBLANES), :]
                              for c in range(SUBLANES)], axis=-1)
        acc = gates[:, k:k + 1] * yk if acc is None else acc + gates[:, k:k + 1] * yk
    x = x_ref[...] + g2_ref[...] * acc
    if final:
        x = x * lax.rsqrt(jnp.mean(x * x, axis=-1, keepdims=True) + EPS) * gf_ref[...]
    o_ref[...] = x


def _combine(x, gates, g2, g_final, ys, cond_of_tile, final):
    n, d = x.shape
    tc = COMBINE_TILE
    per_row_tile = ROW_TILE // tc
    return pl.pallas_call(
        functools.partial(_combine_kernel, final=final),
        grid=(n // tc,),
        in_specs=[pl.BlockSpec((tc, d), lambda i: (i, 0)),
                  pl.BlockSpec((tc, TOP_K), lambda i: (i, 0)),
                  pl.BlockSpec((None, 1, d), lambda i: (cond_of_tile(i // per_row_tile), 0, 0)),
                  pl.BlockSpec((1, d), lambda i: (0, 0)),
                  pl.BlockSpec((tc * TOP_K * SUBLANES, LANES), lambda i: (i, 0))],
        out_specs=pl.BlockSpec((tc, d), lambda i: (i, 0)),
        out_shape=jax.ShapeDtypeStruct((n, d), F32),
        compiler_params=_params("arbitrary"),
        name="combine_final" if final else "combine",
    )(x, gates, g2, g_final.reshape(1, d), ys)


def _moe(x, h_tiles, logits, g2, g_final, layer, w_up, b_up, w_down, b_down, cond_of_tile, final):
    n = x.shape[0]
    tm = EXPERT_TILE
    idx, gates, counts = _route(logits)
    n_ids = n * TOP_K
    id_bits = (n_ids + tm - 1).bit_length()
    counts = counts[0, :N_EXPERTS].astype(jnp.int32)
    padded = ((counts + tm - 1) // tm) * tm
    pad_c = jnp.arange(tm, dtype=jnp.int32)
    pad_expert = jnp.where(pad_c[None, :] < (padded - counts)[:, None],
                           jnp.arange(N_EXPERTS, dtype=jnp.int32)[:, None], N_EXPERTS)
    keys = jnp.concatenate([idx.reshape(-1) * (1 << id_bits) + jnp.arange(n_ids, dtype=jnp.int32),
                            (pad_expert * (1 << id_bits) + (n_ids + pad_c)[None, :]).reshape(-1)])
    a_sorted = jnp.sort(keys) & ((1 << id_bits) - 1)
    a_ext = jnp.concatenate([n_ids + pad_c, a_sorted, n_ids + pad_c, n_ids + pad_c]).astype(jnp.int32)
    ends = jnp.cumsum(padded)
    max_tiles = n_ids // tm + N_EXPERTS
    n_tiles = ends[-1] // tm
    tile_rows = jnp.minimum(jnp.arange(max_tiles + 2, dtype=jnp.int32), n_tiles - 1) * tm
    tile_expert = jnp.minimum(jnp.sum(ends[None, :] <= tile_rows[:, None], axis=1), N_EXPERTS - 1).astype(jnp.int32)
    ys = _experts(a_ext, tile_expert, n_tiles.reshape(1).astype(jnp.int32), h_tiles, layer, w_up, b_up, w_down, b_down)
    return _combine(x, gates, g2, g_final, ys, cond_of_tile, final)


def _rope_tables(n_prompt, batch, length):
    rows = jnp.arange(length) // GRID_W
    cols = jnp.arange(length) % GRID_W
    nfreq = HEAD_DIM // 4
    inv = ROPE_BASE ** (-jnp.arange(nfreq, dtype=F32) / nfreq)
    lane = jnp.arange(LANES)
    is_col = (lane % HEAD_DIM) >= HEAD_DIM // 2
    pos = jnp.where(is_col[None, :], cols[:, None], rows[:, None]).astype(F32)
    ang = pos * inv[lane % nfreq][None, :]
    sign = jnp.where((lane % (2 * nfreq)) < nfreq, -1.0, 1.0)[None, :]
    cos = jnp.tile(jnp.cos(ang), (batch, 1))
    sin = jnp.tile(jnp.sin(ang) * sign, (batch, 1))
    cos = jnp.concatenate([jnp.ones((n_prompt, LANES), F32), cos], axis=0)
    sin = jnp.concatenate([jnp.zeros((n_prompt, LANES), F32), sin], axis=0)
    return cos, sin


def _block_diag_pairs(w):
    w = w.reshape(2, 4, 2, RNN_BLOCK, RNN_BLOCK)
    z = jnp.zeros_like(w[:, :, 0])
    top = jnp.concatenate([w[:, :, 0], z], axis=-1)
    bot = jnp.concatenate([z, w[:, :, 1]], axis=-1)
    return jnp.concatenate([top, bot], axis=-2)


def kernel(x_prompt, x_sample, c, cache_k, cache_v, state_h, c_ctx, w_ada, b_ada, g_norm1, g_norm2, g_final, w_in_ab, sink, w_spatial, b_spatial, g_sgu, w_in_cd, conv_c_w, conv_c_b, w_rg_a, b_rg_a, w_rg_i, b_rg_i, lam, conv_d_w, conv_d_b, ln_d_g, ln_d_b, w_out, w_router, b_router, w_up, b_up, w_down, b_down):
    batch, seq, d = x_prompt.shape
    dec_batch, dec_seq, _ = x_sample.shape
    depth = w_ada.shape[0]
    n_p = batch * seq
    n_s = dec_batch * dec_seq
    past = cache_k.shape[2]
    assert n_p % ROW_TILE == 0 and dec_seq % ROW_TILE == 0 and dec_batch + 1 <= 8
    prompt_tiles = n_p // ROW_TILE
    tiles_per_seq = dec_seq // ROW_TILE

    def cond_of_tile(i):
        return jnp.where(i < prompt_tiles, 0, 1 + (i - prompt_tiles) // tiles_per_seq)

    x = jnp.concatenate([x_prompt.reshape(n_p, d), x_sample.reshape(n_s, d)], axis=0)
    cond8 = jnp.zeros((8, d), F32).at[0].set(c_ctx).at[1:1 + dec_batch].set(c)
    mods = _adaln(cond8, w_ada, b_ada)
    mods = mods.reshape(depth, 8, N_MOD, d).transpose(0, 2, 1, 3)[:, :, :, None, :]
    tables = _rope_tables(n_p, dec_batch, dec_seq)

    new_k, new_v, new_h = [], [], []
    for l in range(depth):
        j = l // 2
        sh1, sc1, g1, sh2, sc2, g2 = (mods[l, m] for m in range(N_MOD))
        if l % 2 == 0:
            w = w_in_ab[j]
            q_end, k_end, v_end = ATT_WIDTH, ATT_WIDTH + KV_WIDTH, ATT_WIDTH + 2 * KV_WIDTH
            w = jnp.concatenate([w[:, :q_end], w[:, v_end:], w[:, q_end:v_end]], axis=1).astype(BF16)
            proj, kv = _inproj(x, g_norm1[l], sc1, sh1, w, cond_of_tile, tables)
            new_k.append(kv[:n_p, :KV_WIDTH].reshape(batch, seq, N_KV_HEADS, HEAD_DIM))
            new_v.append(kv[:n_p, KV_WIDTH:].reshape(batch, seq, N_KV_HEADS, HEAD_DIM))
            sink_b = jnp.broadcast_to(sink[j][:, None], (N_Q_HEADS, LANES))
            att_p = _ctx_attention(proj, sink_b, batch, seq)
            ctx_k = cache_k[:, j].reshape(dec_batch, past, KV_WIDTH).astype(BF16)
            ctx_v = cache_v[:, j].reshape(dec_batch, past, KV_WIDTH).astype(BF16)
            att_s = _lat_attention(proj, sink_b, ctx_k, ctx_v, n_p, dec_batch, dec_seq)
            mix_a = jnp.concatenate([att_p, att_s], axis=0)
            mix_b = _sgu(proj, g_sgu[j], w_spatial[j].astype(BF16), b_spatial[j].T)
        else:
            proj = _inproj(x, g_norm1[l], sc1, sh1, w_in_cd[j].astype(BF16), cond_of_tile)
            wa = _block_diag_pairs(w_rg_a[j])
            wi = _block_diag_pairs(w_rg_i[j])
            w4 = jnp.concatenate([wa[0], wi[0], wa[1], wi[1]], axis=-1).astype(BF16)
            strips = lambda v: v.reshape(RNN_WIDTH // LANES, 1, LANES)
            b4 = jnp.concatenate([strips(b_rg_a[j, 0]), strips(b_rg_i[j, 0]),
                                  strips(b_rg_a[j, 1]), strips(b_rg_i[j, 1])], axis=-1)
            h0_p = jnp.zeros((batch, 2, RNN_WIDTH), F32)
            c_p, h_fin = _rglru(proj, 0, batch, seq, conv_c_w[j], conv_c_b[j], w4, b4, lam[j], h0_p)
            c_s, _ = _rglru(proj, n_p, dec_batch, dec_seq, conv_c_w[j], conv_c_b[j], w4, b4, lam[j], state_h[:, j])
            new_h.append(h_fin)
            w_pad = jnp.concatenate([conv_d_w[j], jnp.zeros((4 * SUBLANES - CONV_K, CONV_WIDTH), F32)], axis=0)
            d_p = _convmod(proj, 0, batch, seq, w_pad, conv_d_b[j], ln_d_g[j], ln_d_b[j])
            d_s = _convmod(proj, n_p, dec_batch, dec_seq, w_pad, conv_d_b[j], ln_d_g[j], ln_d_b[j])
            mix_a = jnp.concatenate([c_p, c_s], axis=0)
            mix_b = jnp.concatenate([d_p, d_s], axis=0)
        w_r = jnp.concatenate([w_router[l], jnp.zeros((d, LANES - N_EXPERTS), F32)], axis=1)
        w_r_hi = w_r.astype(BF16)
        w_r = jnp.concatenate([w_r_hi, (w_r - w_r_hi.astype(F32)).astype(BF16)], axis=1)
        b_r = jnp.concatenate([b_router[l], jnp.full((LANES - N_EXPERTS,), NEG_INF, F32)]).reshape(1, LANES)
        x, h_tiles, logits = _outproj(x, mix_a, mix_b, w_out[l].astype(BF16), g1, g_norm2[l], sc2, sh2, w_r, b_r, cond_of_tile)
        x = _moe(x, h_tiles, logits, g2, g_final, l, w_up, b_up, w_down, b_down, cond_of_tile, final=(l == depth - 1))

    y_prompt = x[:n_p].reshape(batch, seq, d)
    y_sample = x[n_p:].reshape(dec_batch, dec_seq, d)
    return (y_prompt, y_sample, jnp.stack(new_k, axis=1), jnp.stack(new_v, axis=1), jnp.stack(new_h, axis=1))
```

```python
import functools

import jax
import jax.numpy as jnp
from jax import lax
from jax.experimental import pallas as pl
from jax.experimental.pallas import tpu as pltpu

F32 = jnp.float32
BF16 = jnp.bfloat16

HEAD_DIM = 64
N_Q_HEADS = 8
N_KV_HEADS = 2
Q_PER_KV = N_Q_HEADS // N_KV_HEADS
ATT_WIDTH = N_Q_HEADS * HEAD_DIM
KV_WIDTH = N_KV_HEADS * HEAD_DIM
BLOCK = 128
GRID_W = 64
ROPE_BASE = 10000.0
N_SGU_GROUPS = 8
SGU_GROUP = 64
SGU_WIDTH = N_SGU_GROUPS * SGU_GROUP
CHUNK = 128
RNN_BLOCK = 64
RNN_WIDTH = 512
RNN_CONV = 4
RG_C = 8.0
CONV_WIDTH = 512
CONV_K = 31
N_EXPERTS = 32
TOP_K = 4
SWIGLU_LIMIT = 7.0
SWIGLU_ALPHA = 1.702
N_MOD = 6
EPS = 1e-6
NEG_INF = -1e30

LANES = 128
SUBLANES = 8
VMEM_LIMIT = 52 * 1024 * 1024

ROW_TILE = 512
EXPERT_TILE = 256
COMBINE_TILE = 256
COPY_UNROLL = 8
FF_CHUNKS = 4
ID_BITS = 17
CONV_HALO = 16


def _params(*sem):
    return pltpu.CompilerParams(dimension_semantics=sem, vmem_limit_bytes=VMEM_LIMIT)


def _gelu(x):
    return jax.nn.gelu(x, approximate=True)


def _rms_mod(x, g, sc, sh):
    h = x * lax.rsqrt(jnp.mean(x * x, axis=-1, keepdims=True) + EPS) * g
    return h * (1.0 + sc) + sh


def _adaln_kernel(c_ref, w_ref, b_ref, o_ref):
    c = c_ref[...]
    s = c * jax.nn.sigmoid(c)
    o_ref[...] = jnp.dot(s.astype(BF16), w_ref[...].astype(BF16), preferred_element_type=F32) + b_ref[...]


def _adaln(cond8, w_ada, b_ada):
    depth, d, n = w_ada.shape
    tn = n // 4
    return pl.pallas_call(
        _adaln_kernel,
        grid=(depth, n // tn),
        in_specs=[pl.BlockSpec((8, d), lambda l, j: (0, 0)),
                  pl.BlockSpec((None, d, tn), lambda l, j: (l, 0, j)),
                  pl.BlockSpec((None, 1, tn), lambda l, j: (l, 0, j))],
        out_specs=pl.BlockSpec((None, 8, tn), lambda l, j: (l, 0, j)),
        out_shape=jax.ShapeDtypeStruct((depth, 8, n), F32),
        compiler_params=_params("arbitrary", "arbitrary"),
        name="adaln",
    )(cond8, w_ada, b_ada.reshape(depth, 1, n))


def _inproj_kernel(x_ref, g_ref, sc_ref, sh_ref, w_ref, *rest, rope):
    h = _rms_mod(x_ref[...], g_ref[...], sc_ref[...], sh_ref[...])
    acc = jnp.dot(h.astype(BF16), w_ref[...], preferred_element_type=F32)
    if not rope:
        (o_ref,) = rest
        o_ref[...] = acc.astype(BF16)
        return
    cos_ref, sin_ref, o_ref, kv_ref = rest
    kv0 = ATT_WIDTH + 2 * SGU_WIDTH
    kv_ref[...] = acc[:, kv0:kv0 + 2 * KV_WIDTH]
    cos = cos_ref[...]
    sin = sin_ref[...]
    lane = lax.broadcasted_iota(jnp.int32, cos.shape, 1)
    first = (lane % 32) < 16
    o_ref[...] = acc.astype(BF16)
    for j in (0, 1, 2, 3, kv0 // LANES):
        seg = acc[:, LANES * j:LANES * (j + 1)]
        partner = jnp.where(first, pltpu.roll(seg, LANES - 16, 1), pltpu.roll(seg, 16, 1))
        o_ref[:, LANES * j:LANES * (j + 1)] = (seg * cos + partner * sin).astype(BF16)


def _inproj(x, g, sc, sh, w, cond_of_tile, tables=None):
    n, d = x.shape
    nout = w.shape[1]
    tm = ROW_TILE
    rope = tables is not None
    in_specs = [pl.BlockSpec((tm, d), lambda i: (i, 0)),
                pl.BlockSpec((1, d), lambda i: (0, 0)),
                pl.BlockSpec((None, 1, d), lambda i: (cond_of_tile(i), 0, 0)),
                pl.BlockSpec((None, 1, d), lambda i: (cond_of_tile(i), 0, 0)),
                pl.BlockSpec((d, nout), lambda i: (0, 0))]
    args = [x, g.reshape(1, d), sc, sh, w]
    out_specs = pl.BlockSpec((tm, nout), lambda i: (i, 0))
    out_shape = jax.ShapeDtypeStruct((n, nout), BF16)
    if rope:
        in_specs += [pl.BlockSpec((tm, LANES), lambda i: (i, 0))] * 2
        args += list(tables)
        out_specs = [out_specs, pl.BlockSpec((tm, 2 * KV_WIDTH), lambda i: (i, 0))]
        out_shape = [out_shape, jax.ShapeDtypeStruct((n, 2 * KV_WIDTH), F32)]
    return pl.pallas_call(
        functools.partial(_inproj_kernel, rope=rope),
        grid=(n // tm,),
        in_specs=in_specs, out_specs=out_specs, out_shape=out_shape,
        compiler_params=_params("arbitrary"),
        name="inproj_rope" if rope else "inproj",
    )(*args)


def _attend(qh, keys, values, masks, sink):
    scale = HEAD_DIM ** -0.5
    scores = []
    for kk, mk in zip(keys, masks):
        s = lax.dot_general(qh, kk, (((1,), (1,)), ((), ())), preferred_element_type=F32) * scale
        if mk is not None:
            s = jnp.where(mk, s, NEG_INF)
        scores.append(s)
    m = sink
    for s in scores:
        m = jnp.maximum(m, jnp.max(s, axis=-1, keepdims=True))
    es = [jnp.exp(s - m) for s in scores]
    den = jnp.exp(sink - m)
    for e in es:
        den = den + jnp.sum(e, axis=-1, keepdims=True)
    inv = 1.0 / den
    out = None
    for e, vv in zip(es, values):
        o = jnp.dot((e * inv).astype(BF16), vv, preferred_element_type=F32)
        out = o if out is None else out + o
    return out


def _ctx_attn_kernel(sink_ref, q_ref, kv_ref, o_ref):
    q = q_ref[...]
    kv = kv_ref[...]
    for h in range(N_Q_HEADS):
        g = h // Q_PER_KV
        qh = q[:, HEAD_DIM * h:HEAD_DIM * (h + 1)]
        kg = kv[:, HEAD_DIM * g:HEAD_DIM * (g + 1)]
        vg = kv[:, KV_WIDTH + HEAD_DIM * g:KV_WIDTH + HEAD_DIM * (g + 1)]
        out = _attend(qh, [kg], [vg], [None], sink_ref[h:h + 1, 0:1])
        o_ref[:, HEAD_DIM * h:HEAD_DIM * (h + 1)] = out.astype(BF16)


def _ctx_attention(proj, sink_b, batch, seq):
    kv_blk = (ATT_WIDTH + 2 * SGU_WIDTH) // (2 * KV_WIDTH)
    return pl.pallas_call(
        _ctx_attn_kernel,
        grid=(batch,),
        in_specs=[pl.BlockSpec((N_Q_HEADS, LANES), lambda b: (0, 0)),
                  pl.BlockSpec((seq, ATT_WIDTH), lambda b: (b, 0)),
                  pl.BlockSpec((seq, 2 * KV_WIDTH), lambda b: (b, kv_blk))],
        out_specs=pl.BlockSpec((seq, ATT_WIDTH), lambda b: (b, 0)),
        out_shape=jax.ShapeDtypeStruct((batch * seq, ATT_WIDTH), BF16),
        compiler_params=_params("arbitrary"),
        name="ctx_attention",
    )(sink_b, proj, proj)


def _lat_attn_kernel(sink_ref, q_ref, kvp_ref, kvc_ref, kvn_ref, ck_ref, cv_ref, o_ref):
    n = pl.program_id(1)
    nb = pl.num_programs(1)
    q = q_ref[...]
    kvw = jnp.concatenate([kvp_ref[...], kvc_ref[...], kvn_ref[...]], axis=0)
    ck = ck_ref[...]
    cv = cv_ref[...]
    qi = lax.broadcasted_iota(jnp.int32, (BLOCK, 3 * BLOCK), 0)
    kj = lax.broadcasted_iota(jnp.int32, (BLOCK, 3 * BLOCK), 1)
    lo = jnp.where(n > 0, 0, BLOCK)
    hi = jnp.where(n < nb - 1, 3 * BLOCK, 2 * BLOCK)
    valid = (kj >= qi) & (kj <= qi + 2 * BLOCK) & (kj >= lo) & (kj < hi)
    for h in range(N_Q_HEADS):
        g = h // Q_PER_KV
        qh = q[:, HEAD_DIM * h:HEAD_DIM * (h + 1)]
        kw = kvw[:, HEAD_DIM * g:HEAD_DIM * (g + 1)]
        vw = kvw[:, KV_WIDTH + HEAD_DIM * g:KV_WIDTH + HEAD_DIM * (g + 1)]
        kc = ck[:, HEAD_DIM * g:HEAD_DIM * (g + 1)]
        vc = cv[:, HEAD_DIM * g:HEAD_DIM * (g + 1)]
        out = _attend(qh, [kw, kc], [vw, vc], [valid, None], sink_ref[h:h + 1, 0:1])
        o_ref[:, HEAD_DIM * h:HEAD_DIM * (h + 1)] = out.astype(BF16)


def _lat_attention(proj, sink_b, ctx_k, ctx_v, row0, batch, length):
    nb = length // BLOCK
    b0 = row0 // BLOCK
    kv_blk = (ATT_WIDTH + 2 * SGU_WIDTH) // (2 * KV_WIDTH)
    past = ctx_k.shape[1]
    kv_spec = lambda f: pl.BlockSpec((BLOCK, 2 * KV_WIDTH), lambda b, n: (b0 + b * nb + f(n), kv_blk))
    return pl.pallas_call(
        _lat_attn_kernel,
        grid=(batch, nb),
        in_specs=[pl.BlockSpec((N_Q_HEADS, LANES), lambda b, n: (0, 0)),
                  pl.BlockSpec((BLOCK, ATT_WIDTH), lambda b, n: (b0 + b * nb + n, 0)),
                  kv_spec(lambda n: jnp.maximum(n - 1, 0)),
                  kv_spec(lambda n: n),
                  kv_spec(lambda n: jnp.minimum(n + 1, nb - 1)),
                  pl.BlockSpec((None, past, KV_WIDTH), lambda b, n: (b, 0, 0)),
                  pl.BlockSpec((None, past, KV_WIDTH), lambda b, n: (b, 0, 0))],
        out_specs=pl.BlockSpec((BLOCK, ATT_WIDTH), lambda b, n: (b * nb + n, 0)),
        out_shape=jax.ShapeDtypeStruct((batch * length, ATT_WIDTH), BF16),
        compiler_params=_params("arbitrary", "arbitrary"),
        name="latent_attention",
    )(sink_b, proj, proj, proj, proj, ctx_k, ctx_v)


def _sgu_kernel(u_ref, v_ref, g_ref, ws_ref, bs_ref, o_ref):
    rows = u_ref.shape[0]
    v = _gelu(v_ref[...].astype(F32))
    mu = jnp.mean(v, axis=-1, keepdims=True)
    vc = v - mu
    v = vc * lax.rsqrt(jnp.mean(vc * vc, axis=-1, keepdims=True) + EPS) * g_ref[...]
    vb = v.astype(BF16)
    for c in range(rows // CHUNK):
        r0 = c * CHUNK
        for g in range(N_SGU_GROUPS):
            c0 = g * SGU_GROUP
            mixed = jnp.dot(ws_ref[g], vb[r0:r0 + CHUNK, c0:c0 + SGU_GROUP], preferred_element_type=F32)
            mixed = mixed + bs_ref[:, g:g + 1]
            u = _gelu(u_ref[r0:r0 + CHUNK, c0:c0 + SGU_GROUP].astype(F32))
            o_ref[r0:r0 + CHUNK, c0:c0 + SGU_GROUP] = (u * mixed).astype(BF16)


def _sgu(proj, g_sgu, w_s, b_s_t):
    n = proj.shape[0]
    tm = 2 * CHUNK
    return pl.pallas_call(
        _sgu_kernel,
        grid=(n // tm,),
        in_specs=[pl.BlockSpec((tm, SGU_WIDTH), lambda i: (i, 1)),
                  pl.BlockSpec((tm, SGU_WIDTH), lambda i: (i, 2)),
                  pl.BlockSpec((1, SGU_WIDTH), lambda i: (0, 0)),
                  pl.BlockSpec((N_SGU_GROUPS, CHUNK, CHUNK), lambda i: (0, 0, 0)),
                  pl.BlockSpec((CHUNK, N_SGU_GROUPS), lambda i: (0, 0))],
        out_specs=pl.BlockSpec((tm, SGU_WIDTH), lambda i: (i, 0)),
        out_shape=jax.ShapeDtypeStruct((n, SGU_WIDTH), BF16),
        compiler_params=_params("arbitrary"),
        name="sgu",
    )(proj, proj, g_sgu.reshape(1, SGU_WIDTH), w_s, b_s_t)


def _scan8(a, b, reverse):
    row = lax.broadcasted_iota(jnp.int32, a.shape, 0)
    for d in (1, 2, 4):
        if reverse:
            keep = row < SUBLANES - d
            shift = SUBLANES - d
        else:
            keep = row >= d
            shift = d
        a_sh = jnp.where(keep, pltpu.roll(a, shift, 0), 1.0)
        b_sh = jnp.where(keep, pltpu.roll(b, shift, 0), 0.0)
        b = b + a * b_sh
        a = a * a_sh
    return a, b


def _rglru_kernel(gate_ref, xr_ref, cw_ref, cb_ref, w4_ref, b4_ref, lam_ref, h0_ref, o_ref, hl_ref,
                  xpad, a_f, b_f, a_b, b_b, y_f, y_b):
    length = xr_ref.shape[0]
    xpad[0:SUBLANES, :] = jnp.zeros((SUBLANES, LANES), F32)
    xpad[SUBLANES:SUBLANES + length, :] = xr_ref[...].astype(F32)
    xpad[SUBLANES + length:, :] = jnp.zeros((SUBLANES, LANES), F32)
    left = RNN_CONV // 2
    xc = cb_ref[...] + cw_ref[0:1, :] * xpad[SUBLANES - left:SUBLANES - left + length, :]
    for j in range(1, RNN_CONV):
        xc = xc + cw_ref[j:j + 1, :] * xpad[SUBLANES - left + j:SUBLANES - left + j + length, :]
    pre = jnp.dot(xc.astype(BF16), w4_ref[...], preferred_element_type=F32) + b4_ref[...]
    lam = lam_ref[...]
    sp = jnp.maximum(-lam, 0.0) + jnp.log(1.0 + jnp.exp(-jnp.abs(lam)))
    for d, (a_ref, b_ref) in enumerate(((a_f, b_f), (a_b, b_b))):
        r = jax.nn.sigmoid(pre[:, 2 * d * LANES:(2 * d + 1) * LANES])
        i = jax.nn.sigmoid(pre[:, (2 * d + 1) * LANES:(2 * d + 2) * LANES])
        log_a = -RG_C * r * sp[d:d + 1, :]
        a = jnp.exp(log_a)
        a_ref[...] = a
        b_ref[...] = jnp.sqrt(1.0 - a * a) * (i * xc)

    groups = length // SUBLANES

    def body(g, carry):
        hf, hb = carry
        rf = pl.multiple_of(g * SUBLANES, SUBLANES)
        rb = pl.multiple_of((groups - 1 - g) * SUBLANES, SUBLANES)
        af, bf = _scan8(a_f[pl.ds(rf, SUBLANES), :], b_f[pl.ds(rf, SUBLANES), :], False)
        ab, bb = _scan8(a_b[pl.ds(rb, SUBLANES), :], b_b[pl.ds(rb, SUBLANES), :], True)
        yf = bf + af * hf
        yb = bb + ab * hb
        y_f[pl.ds(rf, SUBLANES), :] = yf
        y_b[pl.ds(rb, SUBLANES), :] = yb
        return yf[SUBLANES - 1:SUBLANES, :], yb[0:1, :]

    hf, hb = lax.fori_loop(0, groups, body, (h0_ref[0:1, :], h0_ref[1:2, :]))
    hl_ref[0:1, :] = hf
    hl_ref[1:2, :] = hb
    o_ref[...] = (_gelu(gate_ref[...].astype(F32)) * (y_f[...] + y_b[...])).astype(BF16)


def _rglru(proj, row0, batch, length, conv_w, conv_b, w4, b4, lam, h0):
    strips = RNN_WIDTH // LANES
    r0 = row0 // length
    seq = lambda c0: pl.BlockSpec((length, LANES), lambda b, j: (r0 + b, c0 + j))
    strip = lambda rows: pl.BlockSpec((rows, LANES), lambda b, j: (0, j))
    scratch = [pltpu.VMEM((length + 2 * SUBLANES, LANES), F32)] + [pltpu.VMEM((length, LANES), F32)] * 6
    return pl.pallas_call(
        _rglru_kernel,
        grid=(batch, strips),
        in_specs=[seq(0), seq(strips), strip(RNN_CONV), strip(1),
                  pl.BlockSpec((None, LANES, 4 * LANES), lambda b, j: (j, 0, 0)),
                  pl.BlockSpec((None, 1, 4 * LANES), lambda b, j: (j, 0, 0)),
                  strip(2),
                  pl.BlockSpec((None, 2, LANES), lambda b, j: (b, 0, j))],
        out_specs=[pl.BlockSpec((length, LANES), lambda b, j: (b, j)),
                   pl.BlockSpec((None, 2, LANES), lambda b, j: (b, 0, j))],
        out_shape=[jax.ShapeDtypeStruct((batch * length, RNN_WIDTH), BF16),
                   jax.ShapeDtypeStruct((batch, 2, RNN_WIDTH), F32)],
        scratch_shapes=scratch,
        compiler_params=_params("arbitrary", "arbitrary"),
        name="rglru",
    )(proj, proj, conv_w, conv_b.reshape(1, RNN_WIDTH), w4, b4, lam, h0)


def _convmod_kernel(a_ref, b_ref, pa_ref, pb_ref, na_ref, nb_ref, w_ref, cb_ref, g_ref, beta_ref, o_ref, zpad):
    t = pl.program_id(1)
    nt = pl.num_programs(1)
    rows = a_ref.shape[0]

    def glu(x_ref, y_ref):
        return x_ref[...].astype(F32) * jax.nn.sigmoid(y_ref[...].astype(F32))

    zpad[0:CONV_HALO, :] = glu(pa_ref, pb_ref) * (t > 0).astype(F32)
    zpad[CONV_HALO:CONV_HALO + rows, :] = glu(a_ref, b_ref)
    zpad[CONV_HALO + rows:, :] = glu(na_ref, nb_ref) * (t < nt - 1).astype(F32)
    off = CONV_HALO - CONV_K // 2
    acc = cb_ref[...] + w_ref[0:1, :] * zpad[off:off + rows, :]
    for j in range(1, CONV_K):
        acc = acc + w_ref[j:j + 1, :] * zpad[off + j:off + j + rows, :]
    mu = jnp.mean(acc, axis=-1, keepdims=True)
    xc = acc - mu
    y = xc * lax.rsqrt(jnp.mean(xc * xc, axis=-1, keepdims=True) + EPS) * g_ref[...] + beta_ref[...]
    o_ref[...] = (y * jax.nn.sigmoid(y)).astype(BF16)


def _convmod(proj, row0, batch, length, w_pad, cb, ln_g, ln_b):
    tl = min(length, ROW_TILE)
    nt = length // tl
    r0 = row0 // tl
    hb = tl // CONV_HALO
    h0 = row0 // CONV_HALO
    nh = (batch * length + row0) // CONV_HALO
    ca, cbk = 2, 3
    main = lambda c: pl.BlockSpec((tl, CONV_WIDTH), lambda b, t: (r0 + b * nt + t, c))
    prev = lambda c: pl.BlockSpec((CONV_HALO, CONV_WIDTH),
                                  lambda b, t: (jnp.maximum(h0 + (b * nt + t) * hb - 1, 0), c))
    nxt = lambda c: pl.BlockSpec((CONV_HALO, CONV_WIDTH),
                                 lambda b, t: (jnp.minimum(h0 + (b * nt + t + 1) * hb, nh - 1), c))
    vec = pl.BlockSpec((1, CONV_WIDTH), lambda b, t: (0, 0))
    return pl.pallas_call(
        _convmod_kernel,
        grid=(batch, nt),
        in_specs=[main(ca), main(cbk), prev(ca), prev(cbk), nxt(ca), nxt(cbk),
                  pl.BlockSpec((4 * SUBLANES, CONV_WIDTH), lambda b, t: (0, 0)), vec, vec, vec],
        out_specs=pl.BlockSpec((tl, CONV_WIDTH), lambda b, t: (b * nt + t, 0)),
        out_shape=jax.ShapeDtypeStruct((batch * length, CONV_WIDTH), BF16),
        scratch_shapes=[pltpu.VMEM((tl + 2 * CONV_HALO, CONV_WIDTH), F32)],
        compiler_params=_params("arbitrary", "arbitrary"),
        name="convmod",
    )(proj, proj, proj, proj, proj, proj, w_pad, cb.reshape(1, -1), ln_g.reshape(1, -1), ln_b.reshape(1, -1))


def _store_row_tiles(ref, val):
    rows, d = val.shape
    seg = d // LANES
    for s in range(seg):
        ref[pl.ds(s, rows, stride=seg), :] = val[:, LANES * s:LANES * (s + 1)]


def _load_row_tiles(ref, rows, seg, base=0):
    return jnp.concatenate([ref[pl.ds(base + s, rows, stride=seg), :] for s in range(seg)], axis=-1)


def _outproj_kernel(x_ref, ma_ref, mb_ref, wo_ref, g1_ref, gn_ref, sc_ref, sh_ref, wr_ref, br_ref,
                    xo_ref, h_ref, lg_ref):
    half = ma_ref.shape[1]
    y = jnp.dot(ma_ref[...], wo_ref[0:half, :], preferred_element_type=F32)
    y = y + jnp.dot(mb_ref[...], wo_ref[half:, :], preferred_element_type=F32)
    x = x_ref[...] + g1_ref[...] * y
    xo_ref[...] = x
    h = _rms_mod(x, gn_ref[...], sc_ref[...], sh_ref[...])
    _store_row_tiles(h_ref, h)
    h_hi = h.astype(BF16)
    h_lo = (h - h_hi.astype(F32)).astype(BF16)
    r = jnp.dot(h_hi, wr_ref[...], preferred_element_type=F32)
    r = r[:, :LANES] + r[:, LANES:] + jnp.dot(h_lo, wr_ref[:, :LANES], preferred_element_type=F32)
    lg_ref[...] = r + br_ref[...]


def _outproj(x, mix_a, mix_b, w_out, g1, gn, sc, sh, w_r, b_r, cond_of_tile):
    n, d = x.shape
    tm = ROW_TILE
    seg = d // LANES
    half = mix_a.shape[1]
    row = lambda w: pl.BlockSpec((tm, w), lambda i: (i, 0))
    const = lambda r, c: pl.BlockSpec((r, c), lambda i: (0, 0))
    mod = pl.BlockSpec((None, 1, d), lambda i: (cond_of_tile(i), 0, 0))
    return pl.pallas_call(
        _outproj_kernel,
        grid=(n // tm,),
        in_specs=[row(d), row(half), row(half), const(2 * half, d), mod, const(1, d), mod, mod,
                  const(d, 2 * LANES), const(1, LANES)],
        out_specs=[row(d), pl.BlockSpec((tm * seg, LANES), lambda i: (i, 0)), row(LANES)],
        out_shape=[jax.ShapeDtypeStruct((n, d), F32), jax.ShapeDtypeStruct((n * seg, LANES), F32),
                   jax.ShapeDtypeStruct((n, LANES), F32)],
        compiler_params=_params("arbitrary"),
        name="outproj",
    )(x, mix_a, mix_b, w_out, g1, gn.reshape(1, d), sc, sh, w_r, b_r)


def _route_kernel(lg_ref, idx_ref, gate_ref, cnt_ref, carry):
    @pl.when(pl.program_id(0) == 0)
    def _():
        carry[...] = jnp.zeros_like(carry)

    lg = lg_ref[...]
    lane = lax.broadcasted_iota(jnp.int32, lg.shape, 1).astype(F32)
    work = lg
    vals, idxs = [], []
    for _ in range(TOP_K):
        m = jnp.max(work, axis=-1, keepdims=True)
        ik = jnp.min(jnp.where(work == m, lane, float(LANES)), axis=-1, keepdims=True)
        vals.append(m)
        idxs.append(ik)
        work = jnp.where(lane == ik, -jnp.inf, work)
    es = [jnp.exp(v - vals[0]) for v in vals]
    den = es[0] + es[1] + es[2] + es[3]
    onehot = jnp.zeros(lg.shape, F32)
    for k in range(TOP_K):
        idx_ref[:, k:k + 1] = idxs[k].astype(jnp.int32)
        gate_ref[:, k:k + 1] = es[k] / den
        onehot = onehot + (lane == idxs[k]).astype(F32)
    carry[...] = carry[...] + jnp.sum(onehot, axis=0, keepdims=True)
    cnt_ref[...] = carry[...]


def _route(logits):
    n = logits.shape[0]
    tm = ROW_TILE
    small = lambda dt: jax.ShapeDtypeStruct((n, TOP_K), dt)
    return pl.pallas_call(
        _route_kernel,
        grid=(n // tm,),
        in_specs=[pl.BlockSpec((tm, LANES), lambda i: (i, 0))],
        out_specs=[pl.BlockSpec((tm, TOP_K), lambda i: (i, 0))] * 2 + [pl.BlockSpec((1, LANES), lambda i: (0, 0))],
        out_shape=[small(jnp.int32), small(F32), jax.ShapeDtypeStruct((1, LANES), F32)],
        scratch_shapes=[pltpu.VMEM((1, LANES), F32)],
        compiler_params=_params("arbitrary"),
        name="route",
    )(logits)


def _row_tile(ref, r):
    return ref.at[pl.ds(pl.multiple_of(r * SUBLANES, SUBLANES), SUBLANES)]


def _expert_kernel(a_ref, te_ref, nt_ref, h_ref, wu_ref, bu_ref, wd_ref, bd_ref, ys_ref,
                   xbuf0, xbuf1, ybuf0, ybuf1, xb, yacc, wu_bf, wd_bf, sem_g, sem_s):
    j = pl.program_id(0)
    tm = EXPERT_TILE
    n_tok = h_ref.shape[0] // SUBLANES
    n_tiles = nt_ref[0]
    xbufs, ybufs = (xbuf0, xbuf1), (ybuf0, ybuf1)
    per_chunk = tm // FF_CHUNKS
    cw = wd_bf.shape[1]

    def gather(tile, s, i):
        tok = lax.shift_right_logical(a_ref[(tile + 1) * tm + i], ID_BITS)
        return pltpu.make_async_copy(_row_tile(h_ref, tok), _row_tile(xbufs[s], i), sem_g.at[s])

    def scatter(tile, s, i):
        row = a_ref[(tile + 1) * tm + i] & ((1 << ID_BITS) - 1)
        return pltpu.make_async_copy(_row_tile(ybufs[s], i), _row_tile(ys_ref, row), sem_s.at[s])

    def in_loop(make, act):
        def body(g, carry):
            for u in range(COPY_UNROLL):
                act(make(g * COPY_UNROLL + u))
            return carry
        lax.fori_loop(0, tm // COPY_UNROLL, body, 0)

    @pl.when(j == 0)
    def _():
        ybuf1[...] = jnp.zeros_like(ybuf1)
        in_loop(lambda i: gather(0, 0, i), lambda c: c.start())

    changed = jnp.logical_or(j == 0, te_ref[j] != te_ref[jnp.maximum(j - 1, 0)])

    @pl.when(jnp.logical_and(j <= n_tiles, changed))
    def _():
        for c in range(FF_CHUNKS):
            wu_bf[c] = wu_ref[:, c * cw:(c + 1) * cw].astype(BF16)
            wu_bf[FF_CHUNKS + c] = wu_ref[:, (FF_CHUNKS + c) * cw:(FF_CHUNKS + c + 1) * cw].astype(BF16)
            wd_bf[c] = wd_ref[c * cw:(c + 1) * cw, :].astype(BF16)

    for s in range(2):
        @pl.when(jnp.logical_and(j <= n_tiles + 1, j % 2 == s))
        def _():
            in_loop(lambda i: gather(j, s, i), lambda c: c.wait())

            @pl.when(j >= 1)
            def _():
                in_loop(lambda i: scatter(j - 2, s, i), lambda c: c.wait())

        @pl.when(jnp.logical_and(j <= n_tiles, j % 2 == s))
        def _():
            xb[...] = _load_row_tiles(xbufs[s], tm, SUBLANES).astype(BF16)

        for c in range(FF_CHUNKS):
            @pl.when(jnp.logical_and(j <= n_tiles, j % 2 == s))
            def _():
                for u in range(per_chunk):
                    gather(j + 1, 1 - s, c * per_chunk + u).start()
                    scatter(j - 1, 1 - s, c * per_chunk + u).start()
                x = xb[...]
                g = jnp.dot(x, wu_bf[c], preferred_element_type=F32) + bu_ref[c]
                u = jnp.dot(x, wu_bf[FF_CHUNKS + c], preferred_element_type=F32) + bu_ref[FF_CHUNKS + c]
                g = jnp.minimum(g, SWIGLU_LIMIT)
                u = jnp.clip(u, -SWIGLU_LIMIT, SWIGLU_LIMIT)
                act = (u + 1.0) * (g * jax.nn.sigmoid(SWIGLU_ALPHA * g))
                part = jnp.dot(act.astype(BF16), wd_bf[c], preferred_element_type=F32)
                if c == 0:
                    yacc[...] = part + bd_ref[...]
                elif c < FF_CHUNKS - 1:
                    yacc[...] += part
                else:
                    _store_row_tiles(ybufs[s], yacc[...] + part)


def _experts(a_ext, tile_expert, n_tiles, h_tiles, layer, w_up, b_up, w_down, b_down):
    depth, n_exp, d, ff2 = w_up.shape
    ff = w_down.shape[2]
    tm = EXPERT_TILE
    n_tok = h_tiles.shape[0] // SUBLANES
    pick = lambda j, a, te, nt: (layer, te[j], 0, 0)
    cw = ff // FF_CHUNKS
    tile_buf = pltpu.VMEM((tm * SUBLANES, LANES), F32)
    grid_spec = pltpu.PrefetchScalarGridSpec(
        num_scalar_prefetch=3, grid=(tile_expert.shape[0],),
        in_specs=[pl.BlockSpec(memory_space=pl.ANY),
                  pl.BlockSpec((None, None, d, ff2), pick),
                  pl.BlockSpec((None, None, 2 * FF_CHUNKS, 1, cw), lambda j, a, te, nt: (layer, te[j], 0, 0, 0)),
                  pl.BlockSpec((None, None, ff, d), pick),
                  pl.BlockSpec((None, None, 1, d), pick)],
        out_specs=pl.BlockSpec(memory_space=pl.ANY),
        scratch_shapes=[tile_buf, tile_buf, tile_buf, tile_buf,
                        pltpu.VMEM((tm, d), BF16), pltpu.VMEM((tm, d), F32),
                        pltpu.VMEM((2 * FF_CHUNKS, d, cw), BF16), pltpu.VMEM((FF_CHUNKS, cw, d), BF16),
                        pltpu.SemaphoreType.DMA((2,)), pltpu.SemaphoreType.DMA((2,))])
    return pl.pallas_call(
        _expert_kernel,
        grid_spec=grid_spec,
        out_shape=jax.ShapeDtypeStruct(((TOP_K * n_tok + tm) * SUBLANES, LANES), F32),
        compiler_params=pltpu.CompilerParams(dimension_semantics=("arbitrary",), has_side_effects=True,
                                             vmem_limit_bytes=VMEM_LIMIT),
        name="experts",
    )(a_ext, tile_expert, n_tiles, h_tiles,
      w_up, b_up.reshape(depth, n_exp, 2 * FF_CHUNKS, 1, cw), w_down, b_down.reshape(depth, n_exp, 1, d))


def _combine_kernel(x_ref, gate_ref, g2_ref, gf_ref, y_ref, o_ref, *, final):
    tc = x_ref.shape[0]
    gates = gate_ref[...]
    acc = gates[:, 0:1] * _load_row_tiles(y_ref, tc, SUBLANES)
    for k in range(1, TOP_K):
        acc = acc + gates[:, k:k + 1] * _load_row_tiles(y_ref, tc, SUBLANES, base=k * tc * SUBLANES)
    x = x_ref[...] + g2_ref[...] * acc
    if final:
        x = x * lax.rsqrt(jnp.mean(x * x, axis=-1, keepdims=True) + EPS) * gf_ref[...]
    o_ref[...] = x


def _combine(x, gates, g2, g_final, ys, cond_of_tile, final):
    n, d = x.shape
    tc = COMBINE_TILE
    per_row_tile = ROW_TILE // tc
    return pl.pallas_call(
        functools.partial(_combine_kernel, final=final),
        grid=(n // tc,),
        in_specs=[pl.BlockSpec((tc, d), lambda i: (i, 0)),
                  pl.BlockSpec((tc, TOP_K), lambda i: (i, 0)),
                  pl.BlockSpec((None, 1, d), lambda i: (cond_of_tile(i // per_row_tile), 0, 0)),
                  pl.BlockSpec((1, d), lambda i: (0, 0)),
                  pl.BlockSpec((tc * TOP_K * SUBLANES, LANES), lambda i: (i, 0))],
        out_specs=pl.BlockSpec((tc, d), lambda i: (i, 0)),
        out_shape=jax.ShapeDtypeStruct((n, d), F32),
        compiler_params=_params("arbitrary"),
        name="combine_final" if final else "combine",
    )(x, gates, g2, g_final.reshape(1, d), ys)


def _moe(x, h_tiles, logits, g2, g_final, layer, w_up, b_up, w_down, b_down, cond_of_tile, final):
    n = x.shape[0]
    tm = EXPERT_TILE
    idx, gates, counts = _route(logits)
    n_ids = n * TOP_K
    id_bits = ID_BITS
    assert n_ids + tm <= (1 << ID_BITS) and n <= (1 << (32 - ID_BITS))
    tok = jnp.arange(n, dtype=jnp.int32)[:, None]
    out_row = ((tok // COMBINE_TILE) * (TOP_K * COMBINE_TILE) + jnp.arange(TOP_K, dtype=jnp.int32)[None, :] * COMBINE_TILE
               + tok % COMBINE_TILE)
    counts = counts[0, :N_EXPERTS].astype(jnp.int32)
    padded = ((counts + tm - 1) // tm) * tm
    pad_c = jnp.arange(tm, dtype=jnp.int32)
    pad_expert = jnp.where(pad_c[None, :] < (padded - counts)[:, None],
                           jnp.arange(N_EXPERTS, dtype=jnp.int32)[:, None], N_EXPERTS)
    keys = jnp.concatenate([(idx * (1 << id_bits) + out_row).reshape(-1),
                            (pad_expert * (1 << id_bits) + (n_ids + pad_c)[None, :]).reshape(-1)])
    rows = jnp.sort(keys) & ((1 << id_bits) - 1)
    rows = jnp.concatenate([n_ids + pad_c, rows, n_ids + pad_c, n_ids + pad_c])
    blk = TOP_K * COMBINE_TILE
    src = jnp.minimum((rows // blk) * COMBINE_TILE + rows % COMBINE_TILE, n - 1)
    a_ext = lax.bitcast_convert_type((src.astype(jnp.uint32) << ID_BITS) | rows.astype(jnp.uint32), jnp.int32)
    ends = jnp.cumsum(padded)
    max_tiles = n_ids // tm + N_EXPERTS
    n_tiles = ends[-1] // tm
    tile_rows = jnp.minimum(jnp.arange(max_tiles + 2, dtype=jnp.int32), n_tiles - 1) * tm
    tile_expert = jnp.minimum(jnp.sum(ends[None, :] <= tile_rows[:, None], axis=1), N_EXPERTS - 1).astype(jnp.int32)
    ys = _experts(a_ext, tile_expert, n_tiles.reshape(1).astype(jnp.int32), h_tiles, layer, w_up, b_up, w_down, b_down)
    return _combine(x, gates, g2, g_final, ys, cond_of_tile, final)


def _rope_tables(n_prompt, batch, length):
    rows = jnp.arange(length) // GRID_W
    cols = jnp.arange(length) % GRID_W
    nfreq = HEAD_DIM // 4
    inv = ROPE_BASE ** (-jnp.arange(nfreq, dtype=F32) / nfreq)
    lane = jnp.arange(LANES)
    is_col = (lane % HEAD_DIM) >= HEAD_DIM // 2
    pos = jnp.where(is_col[None, :], cols[:, None], rows[:, None]).astype(F32)
    ang = pos * inv[lane % nfreq][None, :]
    sign = jnp.where((lane % (2 * nfreq)) < nfreq, -1.0, 1.0)[None, :]
    cos = jnp.tile(jnp.cos(ang), (batch, 1))
    sin = jnp.tile(jnp.sin(ang) * sign, (batch, 1))
    cos = jnp.concatenate([jnp.ones((n_prompt, LANES), F32), cos], axis=0)
    sin = jnp.concatenate([jnp.zeros((n_prompt, LANES), F32), sin], axis=0)
    return cos, sin


def _block_diag_pairs(w):
    w = w.reshape(2, 4, 2, RNN_BLOCK, RNN_BLOCK)
    z = jnp.zeros_like(w[:, :, 0])
    top = jnp.concatenate([w[:, :, 0], z], axis=-1)
    bot = jnp.concatenate([z, w[:, :, 1]], axis=-1)
    return jnp.concatenate([top, bot], axis=-2)


def kernel(x_prompt, x_sample, c, cache_k, cache_v, state_h, c_ctx, w_ada, b_ada, g_norm1, g_norm2, g_final, w_in_ab, sink, w_spatial, b_spatial, g_sgu, w_in_cd, conv_c_w, conv_c_b, w_rg_a, b_rg_a, w_rg_i, b_rg_i, lam, conv_d_w, conv_d_b, ln_d_g, ln_d_b, w_out, w_router, b_router, w_up, b_up, w_down, b_down):
    batch, seq, d = x_prompt.shape
    dec_batch, dec_seq, _ = x_sample.shape
    depth = w_ada.shape[0]
    n_p = batch * seq
    n_s = dec_batch * dec_seq
    past = cache_k.shape[2]
    assert n_p % ROW_TILE == 0 and dec_seq % ROW_TILE == 0 and dec_batch + 1 <= 8
    prompt_tiles = n_p // ROW_TILE
    tiles_per_seq = dec_seq // ROW_TILE

    def cond_of_tile(i):
        return jnp.where(i < prompt_tiles, 0, 1 + (i - prompt_tiles) // tiles_per_seq)

    x = jnp.concatenate([x_prompt.reshape(n_p, d), x_sample.reshape(n_s, d)], axis=0)
    cond8 = jnp.zeros((8, d), F32).at[0].set(c_ctx).at[1:1 + dec_batch].set(c)
    mods = _adaln(cond8, w_ada, b_ada)
    mods = mods.reshape(depth, 8, N_MOD, d).transpose(0, 2, 1, 3)[:, :, :, None, :]
    tables = _rope_tables(n_p, dec_batch, dec_seq)

    new_k, new_v, new_h = [], [], []
    for l in range(depth):
        j = l // 2
        sh1, sc1, g1, sh2, sc2, g2 = (mods[l, m] for m in range(N_MOD))
        if l % 2 == 0:
            w = w_in_ab[j]
            q_end, k_end, v_end = ATT_WIDTH, ATT_WIDTH + KV_WIDTH, ATT_WIDTH + 2 * KV_WIDTH
            w = jnp.concatenate([w[:, :q_end], w[:, v_end:], w[:, q_end:v_end]], axis=1).astype(BF16)
            proj, kv = _inproj(x, g_norm1[l], sc1, sh1, w, cond_of_tile, tables)
            new_k.append(kv[:n_p, :KV_WIDTH].reshape(batch, seq, N_KV_HEADS, HEAD_DIM))
            new_v.append(kv[:n_p, KV_WIDTH:].reshape(batch, seq, N_KV_HEADS, HEAD_DIM))
            sink_b = jnp.broadcast_to(sink[j][:, None], (N_Q_HEADS, LANES))
            att_p = _ctx_attention(proj, sink_b, batch, seq)
            ctx_k = cache_k[:, j].reshape(dec_batch, past, KV_WIDTH).astype(BF16)
            ctx_v = cache_v[:, j].reshape(dec_batch, past, KV_WIDTH).astype(BF16)
            att_s = _lat_attention(proj, sink_b, ctx_k, ctx_v, n_p, dec_batch, dec_seq)
            mix_a = jnp.concatenate([att_p, att_s], axis=0)
            mix_b = _sgu(proj, g_sgu[j], w_spatial[j].astype(BF16), b_spatial[j].T)
        else:
            proj = _inproj(x, g_norm1[l], sc1, sh1, w_in_cd[j].astype(BF16), cond_of_tile)
            wa = _block_diag_pairs(w_rg_a[j])
            wi = _block_diag_pairs(w_rg_i[j])
            w4 = jnp.concatenate([wa[0], wi[0], wa[1], wi[1]], axis=-1).astype(BF16)
            strips = lambda v: v.reshape(RNN_WIDTH // LANES, 1, LANES)
            b4 = jnp.concatenate([strips(b_rg_a[j, 0]), strips(b_rg_i[j, 0]),
                                  strips(b_rg_a[j, 1]), strips(b_rg_i[j, 1])], axis=-1)
            h0_p = jnp.zeros((batch, 2, RNN_WIDTH), F32)
            c_p, h_fin = _rglru(proj, 0, batch, seq, conv_c_w[j], conv_c_b[j], w4, b4, lam[j], h0_p)
            c_s, _ = _rglru(proj, n_p, dec_batch, dec_seq, conv_c_w[j], conv_c_b[j], w4, b4, lam[j], state_h[:, j])
            new_h.append(h_fin)
            w_pad = jnp.concatenate([conv_d_w[j], jnp.zeros((4 * SUBLANES - CONV_K, CONV_WIDTH), F32)], axis=0)
            d_p = _convmod(proj, 0, batch, seq, w_pad, conv_d_b[j], ln_d_g[j], ln_d_b[j])
            d_s = _convmod(proj, n_p, dec_batch, dec_seq, w_pad, conv_d_b[j], ln_d_g[j], ln_d_b[j])
            mix_a = jnp.concatenate([c_p, c_s], axis=0)
            mix_b = jnp.concatenate([d_p, d_s], axis=0)
        w_r = jnp.concatenate([w_router[l], jnp.zeros((d, LANES - N_EXPERTS), F32)], axis=1)
        w_r_hi = w_r.astype(BF16)
        w_r = jnp.concatenate([w_r_hi, (w_r - w_r_hi.astype(F32)).astype(BF16)], axis=1)
        b_r = jnp.concatenate([b_router[l], jnp.full((LANES - N_EXPERTS,), NEG_INF, F32)]).reshape(1, LANES)
        x, h_tiles, logits = _outproj(x, mix_a, mix_b, w_out[l].astype(BF16), g1, g_norm2[l], sc2, sh2, w_r, b_r, cond_of_tile)
        x = _moe(x, h_tiles, logits, g2, g_final, l, w_up, b_up, w_down, b_down, cond_of_tile, final=(l == depth - 1))

    y_prompt = x[:n_p].reshape(batch, seq, d)
    y_sample = x[n_p:].reshape(dec_batch, dec_seq, d)
    return (y_prompt, y_sample, jnp.stack(new_k, axis=1), jnp.stack(new_v, axis=1), jnp.stack(new_h, axis=1))
```

```python
import functools

import jax
import jax.numpy as jnp
from jax import lax
from jax.experimental import pallas as pl
from jax.experimental.pallas import tpu as pltpu

F32 = jnp.float32
BF16 = jnp.bfloat16

HEAD_DIM = 64
N_Q_HEADS = 8
N_KV_HEADS = 2
Q_PER_KV = N_Q_HEADS // N_KV_HEADS
ATT_WIDTH = N_Q_HEADS * HEAD_DIM
KV_WIDTH = N_KV_HEADS * HEAD_DIM
BLOCK = 128
GRID_W = 64
ROPE_BASE = 10000.0
N_SGU_GROUPS = 8
SGU_GROUP = 64
SGU_WIDTH = N_SGU_GROUPS * SGU_GROUP
CHUNK = 128
RNN_BLOCK = 64
RNN_WIDTH = 512
RNN_CONV = 4
RG_C = 8.0
CONV_WIDTH = 512
CONV_K = 31
N_EXPERTS = 32
TOP_K = 4
SWIGLU_LIMIT = 7.0
SWIGLU_ALPHA = 1.702
N_MOD = 6
EPS = 1e-6
NEG_INF = -1e30

LANES = 128
SUBLANES = 8
VMEM_LIMIT = 52 * 1024 * 1024

ROW_TILE = 512
EXPERT_TILE = 256
COMBINE_TILE = 256
COPY_UNROLL = 8
FF_CHUNKS = 4
ID_BITS = 17
COPY_SHARE = (0, 96, 192, 256, 256)
CONV_HALO = 16
SCAN_UNROLL = 4


def _params(*sem):
    return pltpu.CompilerParams(dimension_semantics=sem, vmem_limit_bytes=VMEM_LIMIT)


def _gelu(x):
    return jax.nn.gelu(x, approximate=True)


def _rms_mod(x, g, sc, sh):
    h = x * lax.rsqrt(jnp.mean(x * x, axis=-1, keepdims=True) + EPS) * g
    return h * (1.0 + sc) + sh


def _adaln_kernel(c_ref, w_ref, b_ref, o_ref):
    c = c_ref[...]
    s = c * jax.nn.sigmoid(c)
    o_ref[...] = jnp.dot(s.astype(BF16), w_ref[...].astype(BF16), preferred_element_type=F32) + b_ref[...]


def _adaln(cond8, w_ada, b_ada):
    depth, d, n = w_ada.shape
    tn = n // 4
    return pl.pallas_call(
        _adaln_kernel,
        grid=(depth, n // tn),
        in_specs=[pl.BlockSpec((8, d), lambda l, j: (0, 0)),
                  pl.BlockSpec((None, d, tn), lambda l, j: (l, 0, j)),
                  pl.BlockSpec((None, 1, tn), lambda l, j: (l, 0, j))],
        out_specs=pl.BlockSpec((None, 8, tn), lambda l, j: (l, 0, j)),
        out_shape=jax.ShapeDtypeStruct((depth, 8, n), F32),
        compiler_params=_params("arbitrary", "arbitrary"),
        name="adaln",
    )(cond8, w_ada, b_ada.reshape(depth, 1, n))


def _inproj_kernel(x_ref, g_ref, sc_ref, sh_ref, w_ref, *rest, rope):
    h = _rms_mod(x_ref[...], g_ref[...], sc_ref[...], sh_ref[...])
    acc = jnp.dot(h.astype(BF16), w_ref[...], preferred_element_type=F32)
    if not rope:
        (o_ref,) = rest
        o_ref[...] = acc.astype(BF16)
        return
    cos_ref, sin_ref, o_ref, kv_ref = rest
    kv0 = ATT_WIDTH + 2 * SGU_WIDTH
    kv_ref[...] = acc[:, kv0:kv0 + 2 * KV_WIDTH]
    cos = cos_ref[...]
    sin = sin_ref[...]
    lane = lax.broadcasted_iota(jnp.int32, cos.shape, 1)
    first = (lane % 32) < 16
    o_ref[...] = acc.astype(BF16)
    for j in (0, 1, 2, 3, kv0 // LANES):
        seg = acc[:, LANES * j:LANES * (j + 1)]
        partner = jnp.where(first, pltpu.roll(seg, LANES - 16, 1), pltpu.roll(seg, 16, 1))
        o_ref[:, LANES * j:LANES * (j + 1)] = (seg * cos + partner * sin).astype(BF16)


def _inproj(x, g, sc, sh, w, cond_of_tile, tables=None):
    n, d = x.shape
    nout = w.shape[1]
    tm = ROW_TILE
    rope = tables is not None
    in_specs = [pl.BlockSpec((tm, d), lambda i: (i, 0)),
                pl.BlockSpec((1, d), lambda i: (0, 0)),
                pl.BlockSpec((None, 1, d), lambda i: (cond_of_tile(i), 0, 0)),
                pl.BlockSpec((None, 1, d), lambda i: (cond_of_tile(i), 0, 0)),
                pl.BlockSpec((d, nout), lambda i: (0, 0))]
    args = [x, g.reshape(1, d), sc, sh, w]
    out_specs = pl.BlockSpec((tm, nout), lambda i: (i, 0))
    out_shape = jax.ShapeDtypeStruct((n, nout), BF16)
    if rope:
        in_specs += [pl.BlockSpec((tm, LANES), lambda i: (i, 0))] * 2
        args += list(tables)
        out_specs = [out_specs, pl.BlockSpec((tm, 2 * KV_WIDTH), lambda i: (i, 0))]
        out_shape = [out_shape, jax.ShapeDtypeStruct((n, 2 * KV_WIDTH), F32)]
    return pl.pallas_call(
        functools.partial(_inproj_kernel, rope=rope),
        grid=(n // tm,),
        in_specs=in_specs, out_specs=out_specs, out_shape=out_shape,
        compiler_params=_params("arbitrary"),
        name="inproj_rope" if rope else "inproj",
    )(*args)


def _attend(qh, keys, values, masks, sink):
    scale = HEAD_DIM ** -0.5
    scores = []
    for kk, mk in zip(keys, masks):
        s = lax.dot_general(qh, kk, (((1,), (1,)), ((), ())), preferred_element_type=F32) * scale
        if mk is not None:
            s = jnp.where(mk, s, NEG_INF)
        scores.append(s)
    m = sink
    for s in scores:
        m = jnp.maximum(m, jnp.max(s, axis=-1, keepdims=True))
    es = [jnp.exp(s - m) for s in scores]
    den = jnp.exp(sink - m)
    for e in es:
        den = den + jnp.sum(e, axis=-1, keepdims=True)
    inv = 1.0 / den
    out = None
    for e, vv in zip(es, values):
        o = jnp.dot((e * inv).astype(BF16), vv, preferred_element_type=F32)
        out = o if out is None else out + o
    return out


def _ctx_attn_kernel(sink_ref, q_ref, kv_ref, o_ref):
    q = q_ref[...]
    kv = kv_ref[...]
    for h in range(N_Q_HEADS):
        g = h // Q_PER_KV
        qh = q[:, HEAD_DIM * h:HEAD_DIM * (h + 1)]
        kg = kv[:, HEAD_DIM * g:HEAD_DIM * (g + 1)]
        vg = kv[:, KV_WIDTH + HEAD_DIM * g:KV_WIDTH + HEAD_DIM * (g + 1)]
        out = _attend(qh, [kg], [vg], [None], sink_ref[h:h + 1, 0:1])
        o_ref[:, HEAD_DIM * h:HEAD_DIM * (h + 1)] = out.astype(BF16)


def _ctx_attention(proj, sink_b, batch, seq):
    kv_blk = (ATT_WIDTH + 2 * SGU_WIDTH) // (2 * KV_WIDTH)
    return pl.pallas_call(
        _ctx_attn_kernel,
        grid=(batch,),
        in_specs=[pl.BlockSpec((N_Q_HEADS, LANES), lambda b: (0, 0)),
                  pl.BlockSpec((seq, ATT_WIDTH), lambda b: (b, 0)),
                  pl.BlockSpec((seq, 2 * KV_WIDTH), lambda b: (b, kv_blk))],
        out_specs=pl.BlockSpec((seq, ATT_WIDTH), lambda b: (b, 0)),
        out_shape=jax.ShapeDtypeStruct((batch * seq, ATT_WIDTH), BF16),
        compiler_params=_params("arbitrary"),
        name="ctx_attention",
    )(sink_b, proj, proj)


def _lat_attn_kernel(sink_ref, q_ref, kvp_ref, kvc_ref, kvn_ref, ck_ref, cv_ref, o_ref):
    n = pl.program_id(1)
    nb = pl.num_programs(1)
    q = q_ref[...]
    kvw = jnp.concatenate([kvp_ref[...], kvc_ref[...], kvn_ref[...]], axis=0)
    ck = ck_ref[...]
    cv = cv_ref[...]
    qi = lax.broadcasted_iota(jnp.int32, (BLOCK, 3 * BLOCK), 0)
    kj = lax.broadcasted_iota(jnp.int32, (BLOCK, 3 * BLOCK), 1)
    lo = jnp.where(n > 0, 0, BLOCK)
    hi = jnp.where(n < nb - 1, 3 * BLOCK, 2 * BLOCK)
    valid = (kj >= qi) & (kj <= qi + 2 * BLOCK) & (kj >= lo) & (kj < hi)
    for h in range(N_Q_HEADS):
        g = h // Q_PER_KV
        qh = q[:, HEAD_DIM * h:HEAD_DIM * (h + 1)]
        kw = kvw[:, HEAD_DIM * g:HEAD_DIM * (g + 1)]
        vw = kvw[:, KV_WIDTH + HEAD_DIM * g:KV_WIDTH + HEAD_DIM * (g + 1)]
        kc = ck[:, HEAD_DIM * g:HEAD_DIM * (g + 1)]
        vc = cv[:, HEAD_DIM * g:HEAD_DIM * (g + 1)]
        out = _attend(qh, [kw, kc], [vw, vc], [valid, None], sink_ref[h:h + 1, 0:1])
        o_ref[:, HEAD_DIM * h:HEAD_DIM * (h + 1)] = out.astype(BF16)


def _lat_attention(proj, sink_b, ctx_k, ctx_v, row0, batch, length):
    nb = length // BLOCK
    b0 = row0 // BLOCK
    kv_blk = (ATT_WIDTH + 2 * SGU_WIDTH) // (2 * KV_WIDTH)
    past = ctx_k.shape[1]
    kv_spec = lambda f: pl.BlockSpec((BLOCK, 2 * KV_WIDTH), lambda b, n: (b0 + b * nb + f(n), kv_blk))
    return pl.pallas_call(
        _lat_attn_kernel,
        grid=(batch, nb),
        in_specs=[pl.BlockSpec((N_Q_HEADS, LANES), lambda b, n: (0, 0)),
                  pl.BlockSpec((BLOCK, ATT_WIDTH), lambda b, n: (b0 + b * nb + n, 0)),
                  kv_spec(lambda n: jnp.maximum(n - 1, 0)),
                  kv_spec(lambda n: n),
                  kv_spec(lambda n: jnp.minimum(n + 1, nb - 1)),
                  pl.BlockSpec((None, past, KV_WIDTH), lambda b, n: (b, 0, 0)),
                  pl.BlockSpec((None, past, KV_WIDTH), lambda b, n: (b, 0, 0))],
        out_specs=pl.BlockSpec((BLOCK, ATT_WIDTH), lambda b, n: (b * nb + n, 0)),
        out_shape=jax.ShapeDtypeStruct((batch * length, ATT_WIDTH), BF16),
        compiler_params=_params("arbitrary", "arbitrary"),
        name="latent_attention",
    )(sink_b, proj, proj, proj, proj, ctx_k, ctx_v)


def _sgu_kernel(u_ref, v_ref, g_ref, ws_ref, bs_ref, o_ref):
    rows = u_ref.shape[0]
    v = _gelu(v_ref[...].astype(F32))
    mu = jnp.mean(v, axis=-1, keepdims=True)
    vc = v - mu
    v = vc * lax.rsqrt(jnp.mean(vc * vc, axis=-1, keepdims=True) + EPS) * g_ref[...]
    vb = v.astype(BF16)
    for c in range(rows // CHUNK):
        r0 = c * CHUNK
        for g in range(N_SGU_GROUPS):
            c0 = g * SGU_GROUP
            mixed = jnp.dot(ws_ref[g], vb[r0:r0 + CHUNK, c0:c0 + SGU_GROUP], preferred_element_type=F32)
            mixed = mixed + bs_ref[:, g:g + 1]
            u = _gelu(u_ref[r0:r0 + CHUNK, c0:c0 + SGU_GROUP].astype(F32))
            o_ref[r0:r0 + CHUNK, c0:c0 + SGU_GROUP] = (u * mixed).astype(BF16)


def _sgu(proj, g_sgu, w_s, b_s_t):
    n = proj.shape[0]
    tm = 2 * CHUNK
    return pl.pallas_call(
        _sgu_kernel,
        grid=(n // tm,),
        in_specs=[pl.BlockSpec((tm, SGU_WIDTH), lambda i: (i, 1)),
                  pl.BlockSpec((tm, SGU_WIDTH), lambda i: (i, 2)),
                  pl.BlockSpec((1, SGU_WIDTH), lambda i: (0, 0)),
                  pl.BlockSpec((N_SGU_GROUPS, CHUNK, CHUNK), lambda i: (0, 0, 0)),
                  pl.BlockSpec((CHUNK, N_SGU_GROUPS), lambda i: (0, 0))],
        out_specs=pl.BlockSpec((tm, SGU_WIDTH), lambda i: (i, 0)),
        out_shape=jax.ShapeDtypeStruct((n, SGU_WIDTH), BF16),
        compiler_params=_params("arbitrary"),
        name="sgu",
    )(proj, proj, g_sgu.reshape(1, SGU_WIDTH), w_s, b_s_t)


def _scan8(a, b, reverse):
    row = lax.broadcasted_iota(jnp.int32, a.shape, 0)
    for d in (1, 2, 4):
        if reverse:
            keep = row < SUBLANES - d
            shift = SUBLANES - d
        else:
            keep = row >= d
            shift = d
        a_sh = jnp.where(keep, pltpu.roll(a, shift, 0), 1.0)
        b_sh = jnp.where(keep, pltpu.roll(b, shift, 0), 0.0)
        b = b + a * b_sh
        a = a * a_sh
    return a, b


def _rglru_kernel(gate_ref, xr_ref, cw_ref, cb_ref, w4_ref, b4_ref, lam_ref, h0_ref, o_ref, hl_ref,
                  xpad, a_f, b_f, a_b, b_b, y_f, y_b):
    length = xr_ref.shape[0]
    xpad[0:SUBLANES, :] = jnp.zeros((SUBLANES, LANES), F32)
    xpad[SUBLANES:SUBLANES + length, :] = xr_ref[...].astype(F32)
    xpad[SUBLANES + length:, :] = jnp.zeros((SUBLANES, LANES), F32)
    left = RNN_CONV // 2
    xc = cb_ref[...] + cw_ref[0:1, :] * xpad[SUBLANES - left:SUBLANES - left + length, :]
    for j in range(1, RNN_CONV):
        xc = xc + cw_ref[j:j + 1, :] * xpad[SUBLANES - left + j:SUBLANES - left + j + length, :]
    pre = jnp.dot(xc.astype(BF16), w4_ref[...], preferred_element_type=F32) + b4_ref[...]
    lam = lam_ref[...]
    sp = jnp.maximum(-lam, 0.0) + jnp.log(1.0 + jnp.exp(-jnp.abs(lam)))
    for d, (a_ref, b_ref) in enumerate(((a_f, b_f), (a_b, b_b))):
        r = jax.nn.sigmoid(pre[:, 2 * d * LANES:(2 * d + 1) * LANES])
        i = jax.nn.sigmoid(pre[:, (2 * d + 1) * LANES:(2 * d + 2) * LANES])
        log_a = -RG_C * r * sp[d:d + 1, :]
        a = jnp.exp(log_a)
        a_ref[...] = a
        b_ref[...] = jnp.sqrt(1.0 - a * a) * (i * xc)

    groups = length // SUBLANES

    def body(it, carry):
        hf, hb = carry
        for v in range(SCAN_UNROLL):
            g = it * SCAN_UNROLL + v
            rf = pl.multiple_of(g * SUBLANES, SUBLANES)
            rb = pl.multiple_of((groups - 1 - g) * SUBLANES, SUBLANES)
            af, bf = _scan8(a_f[pl.ds(rf, SUBLANES), :], b_f[pl.ds(rf, SUBLANES), :], False)
            ab, bb = _scan8(a_b[pl.ds(rb, SUBLANES), :], b_b[pl.ds(rb, SUBLANES), :], True)
            yf = bf + af * hf
            yb = bb + ab * hb
            y_f[pl.ds(rf, SUBLANES), :] = yf
            y_b[pl.ds(rb, SUBLANES), :] = yb
            hf, hb = yf[SUBLANES - 1:SUBLANES, :], yb[0:1, :]
        return hf, hb

    hf, hb = lax.fori_loop(0, groups // SCAN_UNROLL, body, (h0_ref[0:1, :], h0_ref[1:2, :]))
    hl_ref[0:1, :] = hf
    hl_ref[1:2, :] = hb
    o_ref[...] = (_gelu(gate_ref[...].astype(F32)) * (y_f[...] + y_b[...])).astype(BF16)


def _rglru(proj, row0, batch, length, conv_w, conv_b, w4, b4, lam, h0):
    strips = RNN_WIDTH // LANES
    r0 = row0 // length
    seq = lambda c0: pl.BlockSpec((length, LANES), lambda b, j: (r0 + b, c0 + j))
    strip = lambda rows: pl.BlockSpec((rows, LANES), lambda b, j: (0, j))
    scratch = [pltpu.VMEM((length + 2 * SUBLANES, LANES), F32)] + [pltpu.VMEM((length, LANES), F32)] * 6
    return pl.pallas_call(
        _rglru_kernel,
        grid=(batch, strips),
        in_specs=[seq(0), seq(strips), strip(RNN_CONV), strip(1),
                  pl.BlockSpec((None, LANES, 4 * LANES), lambda b, j: (j, 0, 0)),
                  pl.BlockSpec((None, 1, 4 * LANES), lambda b, j: (j, 0, 0)),
                  strip(2),
                  pl.BlockSpec((None, 2, LANES), lambda b, j: (b, 0, j))],
        out_specs=[pl.BlockSpec((length, LANES), lambda b, j: (b, j)),
                   pl.BlockSpec((None, 2, LANES), lambda b, j: (b, 0, j))],
        out_shape=[jax.ShapeDtypeStruct((batch * length, RNN_WIDTH), BF16),
                   jax.ShapeDtypeStruct((batch, 2, RNN_WIDTH), F32)],
        scratch_shapes=scratch,
        compiler_params=_params("arbitrary", "arbitrary"),
        name="rglru",
    )(proj, proj, conv_w, conv_b.reshape(1, RNN_WIDTH), w4, b4, lam, h0)


def _convmod_kernel(a_ref, b_ref, pa_ref, pb_ref, na_ref, nb_ref, w_ref, cb_ref, g_ref, beta_ref, o_ref, zpad):
    t = pl.program_id(1)
    nt = pl.num_programs(1)
    rows = a_ref.shape[0]

    def glu(x_ref, y_ref):
        return x_ref[...].astype(F32) * jax.nn.sigmoid(y_ref[...].astype(F32))

    zpad[0:CONV_HALO, :] = glu(pa_ref, pb_ref) * (t > 0).astype(F32)
    zpad[CONV_HALO:CONV_HALO + rows, :] = glu(a_ref, b_ref)
    zpad[CONV_HALO + rows:, :] = glu(na_ref, nb_ref) * (t < nt - 1).astype(F32)
    off = CONV_HALO - CONV_K // 2
    acc = cb_ref[...] + w_ref[0:1, :] * zpad[off:off + rows, :]
    for j in range(1, CONV_K):
        acc = acc + w_ref[j:j + 1, :] * zpad[off + j:off + j + rows, :]
    mu = jnp.mean(acc, axis=-1, keepdims=True)
    xc = acc - mu
    y = xc * lax.rsqrt(jnp.mean(xc * xc, axis=-1, keepdims=True) + EPS) * g_ref[...] + beta_ref[...]
    o_ref[...] = (y * jax.nn.sigmoid(y)).astype(BF16)


def _convmod(proj, row0, batch, length, w_pad, cb, ln_g, ln_b):
    tl = min(length, ROW_TILE)
    nt = length // tl
    r0 = row0 // tl
    hb = tl // CONV_HALO
    h0 = row0 // CONV_HALO
    nh = (batch * length + row0) // CONV_HALO
    ca, cbk = 2, 3
    main = lambda c: pl.BlockSpec((tl, CONV_WIDTH), lambda b, t: (r0 + b * nt + t, c))
    prev = lambda c: pl.BlockSpec((CONV_HALO, CONV_WIDTH),
                                  lambda b, t: (jnp.maximum(h0 + (b * nt + t) * hb - 1, 0), c))
    nxt = lambda c: pl.BlockSpec((CONV_HALO, CONV_WIDTH),
                                 lambda b, t: (jnp.minimum(h0 + (b * nt + t + 1) * hb, nh - 1), c))
    vec = pl.BlockSpec((1, CONV_WIDTH), lambda b, t: (0, 0))
    return pl.pallas_call(
        _convmod_kernel,
        grid=(batch, nt),
        in_specs=[main(ca), main(cbk), prev(ca), prev(cbk), nxt(ca), nxt(cbk),
                  pl.BlockSpec((4 * SUBLANES, CONV_WIDTH), lambda b, t: (0, 0)), vec, vec, vec],
        out_specs=pl.BlockSpec((tl, CONV_WIDTH), lambda b, t: (b * nt + t, 0)),
        out_shape=jax.ShapeDtypeStruct((batch * length, CONV_WIDTH), BF16),
        scratch_shapes=[pltpu.VMEM((tl + 2 * CONV_HALO, CONV_WIDTH), F32)],
        compiler_params=_params("arbitrary", "arbitrary"),
        name="convmod",
    )(proj, proj, proj, proj, proj, proj, w_pad, cb.reshape(1, -1), ln_g.reshape(1, -1), ln_b.reshape(1, -1))


def _store_row_tiles(ref, val):
    rows, d = val.shape
    seg = d // LANES
    for s in range(seg):
        ref[pl.ds(s, rows, stride=seg), :] = val[:, LANES * s:LANES * (s + 1)]


def _load_row_tiles(ref, rows, seg, base=0):
    return jnp.concatenate([ref[pl.ds(base + s, rows, stride=seg), :] for s in range(seg)], axis=-1)


def _outproj_kernel(x_ref, ma_ref, mb_ref, wo_ref, g1_ref, gn_ref, sc_ref, sh_ref, wr_ref, br_ref,
                    xo_ref, h_ref, lg_ref):
    half = ma_ref.shape[1]
    y = jnp.dot(ma_ref[...], wo_ref[0:half, :], preferred_element_type=F32)
    y = y + jnp.dot(mb_ref[...], wo_ref[half:, :], preferred_element_type=F32)
    x = x_ref[...] + g1_ref[...] * y
    xo_ref[...] = x
    h = _rms_mod(x, gn_ref[...], sc_ref[...], sh_ref[...])
    _store_row_tiles(h_ref, h)
    h_hi = h.astype(BF16)
    h_lo = (h - h_hi.astype(F32)).astype(BF16)
    r = jnp.dot(h_hi, wr_ref[...], preferred_element_type=F32)
    r = r[:, :LANES] + r[:, LANES:] + jnp.dot(h_lo, wr_ref[:, :LANES], preferred_element_type=F32)
    lg_ref[...] = r + br_ref[...]


def _outproj(x, mix_a, mix_b, w_out, g1, gn, sc, sh, w_r, b_r, cond_of_tile):
    n, d = x.shape
    tm = ROW_TILE
    seg = d // LANES
    half = mix_a.shape[1]
    row = lambda w: pl.BlockSpec((tm, w), lambda i: (i, 0))
    const = lambda r, c: pl.BlockSpec((r, c), lambda i: (0, 0))
    mod = pl.BlockSpec((None, 1, d), lambda i: (cond_of_tile(i), 0, 0))
    return pl.pallas_call(
        _outproj_kernel,
        grid=(n // tm,),
        in_specs=[row(d), row(half), row(half), const(2 * half, d), mod, const(1, d), mod, mod,
                  const(d, 2 * LANES), const(1, LANES)],
        out_specs=[row(d), pl.BlockSpec((tm * seg, LANES), lambda i: (i, 0)), row(LANES)],
        out_shape=[jax.ShapeDtypeStruct((n, d), F32), jax.ShapeDtypeStruct((n * seg, LANES), F32),
                   jax.ShapeDtypeStruct((n, LANES), F32)],
        compiler_params=_params("arbitrary"),
        name="outproj",
    )(x, mix_a, mix_b, w_out, g1, gn.reshape(1, d), sc, sh, w_r, b_r)


def _route_kernel(lg_ref, idx_ref, gate_ref, cnt_ref, carry):
    @pl.when(pl.program_id(0) == 0)
    def _():
        carry[...] = jnp.zeros_like(carry)

    lg = lg_ref[...]
    lane = lax.broadcasted_iota(jnp.int32, lg.shape, 1).astype(F32)
    work = lg
    vals, idxs = [], []
    for _ in range(TOP_K):
        m = jnp.max(work, axis=-1, keepdims=True)
        ik = jnp.min(jnp.where(work == m, lane, float(LANES)), axis=-1, keepdims=True)
        vals.append(m)
        idxs.append(ik)
        work = jnp.where(lane == ik, -jnp.inf, work)
    es = [jnp.exp(v - vals[0]) for v in vals]
    den = es[0] + es[1] + es[2] + es[3]
    onehot = jnp.zeros(lg.shape, F32)
    for k in range(TOP_K):
        idx_ref[:, k:k + 1] = idxs[k].astype(jnp.int32)
        gate_ref[:, k:k + 1] = es[k] / den
        onehot = onehot + (lane == idxs[k]).astype(F32)
    carry[...] = carry[...] + jnp.sum(onehot, axis=0, keepdims=True)
    cnt_ref[...] = carry[...]


def _route(logits):
    n = logits.shape[0]
    tm = ROW_TILE
    small = lambda dt: jax.ShapeDtypeStruct((n, TOP_K), dt)
    return pl.pallas_call(
        _route_kernel,
        grid=(n // tm,),
        in_specs=[pl.BlockSpec((tm, LANES), lambda i: (i, 0))],
        out_specs=[pl.BlockSpec((tm, TOP_K), lambda i: (i, 0))] * 2 + [pl.BlockSpec((1, LANES), lambda i: (0, 0))],
        out_shape=[small(jnp.int32), small(F32), jax.ShapeDtypeStruct((1, LANES), F32)],
        scratch_shapes=[pltpu.VMEM((1, LANES), F32)],
        compiler_params=_params("arbitrary"),
        name="route",
    )(logits)


def _row_tile(ref, r):
    return ref.at[pl.ds(pl.multiple_of(r * SUBLANES, SUBLANES), SUBLANES)]


def _expert_kernel(a_ref, te_ref, nt_ref, h_ref, wu_ref, bu_ref, wd_ref, bd_ref, ys_ref,
                   xbuf0, xbuf1, ybuf0, ybuf1, xb, yacc, wu_bf, wd_bf, sem_g, sem_s):
    j = pl.program_id(0)
    tm = EXPERT_TILE
    n_tok = h_ref.shape[0] // SUBLANES
    n_tiles = nt_ref[0]
    xbufs, ybufs = (xbuf0, xbuf1), (ybuf0, ybuf1)
    assert len(COPY_SHARE) == FF_CHUNKS + 1 and COPY_SHARE[-1] == tm
    cw = wd_bf.shape[1]

    def gather(tile, s, i):
        tok = lax.shift_right_logical(a_ref[(tile + 1) * tm + i], ID_BITS)
        return pltpu.make_async_copy(_row_tile(h_ref, tok), _row_tile(xbufs[s], i), sem_g.at[s])

    def scatter(tile, s, i):
        row = a_ref[(tile + 1) * tm + i] & ((1 << ID_BITS) - 1)
        return pltpu.make_async_copy(_row_tile(ybufs[s], i), _row_tile(ys_ref, row), sem_s.at[s])

    def in_loop(make, act):
        def body(g, carry):
            for u in range(COPY_UNROLL):
                act(make(g * COPY_UNROLL + u))
            return carry
        lax.fori_loop(0, tm // COPY_UNROLL, body, 0)

    @pl.when(j == 0)
    def _():
        ybuf1[...] = jnp.zeros_like(ybuf1)
        in_loop(lambda i: gather(0, 0, i), lambda c: c.start())

    changed = jnp.logical_or(j == 0, te_ref[j] != te_ref[jnp.maximum(j - 1, 0)])

    @pl.when(jnp.logical_and(j <= n_tiles, changed))
    def _():
        for c in range(FF_CHUNKS):
            wu_bf[c] = wu_ref[:, c * cw:(c + 1) * cw].astype(BF16)
            wu_bf[FF_CHUNKS + c] = wu_ref[:, (FF_CHUNKS + c) * cw:(FF_CHUNKS + c + 1) * cw].astype(BF16)
            wd_bf[c] = wd_ref[c * cw:(c + 1) * cw, :].astype(BF16)

    for s in range(2):
        @pl.when(jnp.logical_and(j <= n_tiles + 1, j % 2 == s))
        def _():
            in_loop(lambda i: gather(j, s, i), lambda c: c.wait())

            @pl.when(j >= 1)
            def _():
                in_loop(lambda i: scatter(j - 2, s, i), lambda c: c.wait())

        @pl.when(jnp.logical_and(j <= n_tiles, j % 2 == s))
        def _():
            xb[...] = _load_row_tiles(xbufs[s], tm, SUBLANES).astype(BF16)

        for c in range(FF_CHUNKS):
            @pl.when(jnp.logical_and(jnp.logical_and(j <= n_tiles, j % 2 == s), te_ref[c] >= 0))
            def _():
                for i in range(COPY_SHARE[c], COPY_SHARE[c + 1]):
                    gather(j + 1, 1 - s, i).start()
                    scatter(j - 1, 1 - s, i).start()
                x = xb[...]
                g = jnp.dot(x, wu_bf[c], preferred_element_type=F32) + bu_ref[c]
                u = jnp.dot(x, wu_bf[FF_CHUNKS + c], preferred_element_type=F32) + bu_ref[FF_CHUNKS + c]
                g = jnp.minimum(g, SWIGLU_LIMIT)
                u = jnp.clip(u, -SWIGLU_LIMIT, SWIGLU_LIMIT)
                act = (u + 1.0) * (g * jax.nn.sigmoid(SWIGLU_ALPHA * g))
                part = jnp.dot(act.astype(BF16), wd_bf[c], preferred_element_type=F32)
                if c == 0:
                    yacc[...] = part + bd_ref[...]
                elif c < FF_CHUNKS - 1:
                    yacc[...] += part
                else:
                    _store_row_tiles(ybufs[s], yacc[...] + part)


def _experts(a_ext, tile_expert, n_tiles, h_tiles, layer, w_up, b_up, w_down, b_down):
    depth, n_exp, d, ff2 = w_up.shape
    ff = w_down.shape[2]
    tm = EXPERT_TILE
    n_tok = h_tiles.shape[0] // SUBLANES
    pick = lambda j, a, te, nt: (layer, te[j], 0, 0)
    cw = ff // FF_CHUNKS
    tile_buf = pltpu.VMEM((tm * SUBLANES, LANES), F32)
    grid_spec = pltpu.PrefetchScalarGridSpec(
        num_scalar_prefetch=3, grid=(tile_expert.shape[0],),
        in_specs=[pl.BlockSpec(memory_space=pl.ANY),
                  pl.BlockSpec((None, None, d, ff2), pick),
                  pl.BlockSpec((None, None, 2 * FF_CHUNKS, 1, cw), lambda j, a, te, nt: (layer, te[j], 0, 0, 0)),
                  pl.BlockSpec((None, None, ff, d), pick),
                  pl.BlockSpec((None, None, 1, d), pick)],
        out_specs=pl.BlockSpec(memory_space=pl.ANY),
        scratch_shapes=[tile_buf, tile_buf, tile_buf, tile_buf,
                        pltpu.VMEM((tm, d), BF16), pltpu.VMEM((tm, d), F32),
                        pltpu.VMEM((2 * FF_CHUNKS, d, cw), BF16), pltpu.VMEM((FF_CHUNKS, cw, d), BF16),
                        pltpu.SemaphoreType.DMA((2,)), pltpu.SemaphoreType.DMA((2,))])
    return pl.pallas_call(
        _expert_kernel,
        grid_spec=grid_spec,
        out_shape=jax.ShapeDtypeStruct(((TOP_K * n_tok + tm) * SUBLANES, LANES), F32),
        compiler_params=pltpu.CompilerParams(dimension_semantics=("arbitrary",), has_side_effects=True,
                                             vmem_limit_bytes=VMEM_LIMIT),
        name="experts",
    )(a_ext, tile_expert, n_tiles, h_tiles,
      w_up, b_up.reshape(depth, n_exp, 2 * FF_CHUNKS, 1, cw), w_down, b_down.reshape(depth, n_exp, 1, d))


def _combine_kernel(x_ref, gate_ref, g2_ref, gf_ref, y_ref, o_ref, *, final):
    tc = x_ref.shape[0]
    gates = gate_ref[...]
    acc = gates[:, 0:1] * _load_row_tiles(y_ref, tc, SUBLANES)
    for k in range(1, TOP_K):
        acc = acc + gates[:, k:k + 1] * _load_row_tiles(y_ref, tc, SUBLANES, base=k * tc * SUBLANES)
    x = x_ref[...] + g2_ref[...] * acc
    if final:
        x = x * lax.rsqrt(jnp.mean(x * x, axis=-1, keepdims=True) + EPS) * gf_ref[...]
    o_ref[...] = x


def _combine(x, gates, g2, g_final, ys, cond_of_tile, final):
    n, d = x.shape
    tc = COMBINE_TILE
    per_row_tile = ROW_TILE // tc
    return pl.pallas_call(
        functools.partial(_combine_kernel, final=final),
        grid=(n // tc,),
        in_specs=[pl.BlockSpec((tc, d), lambda i: (i, 0)),
                  pl.BlockSpec((tc, TOP_K), lambda i: (i, 0)),
                  pl.BlockSpec((None, 1, d), lambda i: (cond_of_tile(i // per_row_tile), 0, 0)),
                  pl.BlockSpec((1, d), lambda i: (0, 0)),
                  pl.BlockSpec((tc * TOP_K * SUBLANES, LANES), lambda i: (i, 0))],
        out_specs=pl.BlockSpec((tc, d), lambda i: (i, 0)),
        out_shape=jax.ShapeDtypeStruct((n, d), F32),
        compiler_params=_params("arbitrary"),
        name="combine_final" if final else "combine",
    )(x, gates, g2, g_final.reshape(1, d), ys)


def _moe(x, h_tiles, logits, g2, g_final, layer, w_up, b_up, w_down, b_down, cond_of_tile, final):
    n = x.shape[0]
    tm = EXPERT_TILE
    idx, gates, counts = _route(logits)
    n_ids = n * TOP_K
    id_bits = ID_BITS
    assert n_ids + tm <= (1 << ID_BITS) and n <= (1 << (32 - ID_BITS))
    tok = jnp.arange(n, dtype=jnp.int32)[:, None]
    out_row = ((tok // COMBINE_TILE) * (TOP_K * COMBINE_TILE) + jnp.arange(TOP_K, dtype=jnp.int32)[None, :] * COMBINE_TILE
               + tok % COMBINE_TILE)
    counts = counts[0, :N_EXPERTS].astype(jnp.int32)
    padded = ((counts + tm - 1) // tm) * tm
    pad_c = jnp.arange(tm, dtype=jnp.int32)
    pad_expert = jnp.where(pad_c[None, :] < (padded - counts)[:, None],
                           jnp.arange(N_EXPERTS, dtype=jnp.int32)[:, None], N_EXPERTS)
    keys = jnp.concatenate([(idx * (1 << id_bits) + out_row).reshape(-1),
                            (pad_expert * (1 << id_bits) + (n_ids + pad_c)[None, :]).reshape(-1)])
    rows = jnp.sort(keys) & ((1 << id_bits) - 1)
    rows = jnp.concatenate([n_ids + pad_c, rows, n_ids + pad_c, n_ids + pad_c])
    blk = TOP_K * COMBINE_TILE
    src = jnp.minimum((rows // blk) * COMBINE_TILE + rows % COMBINE_TILE, n - 1)
    a_ext = lax.bitcast_convert_type((src.astype(jnp.uint32) << ID_BITS) | rows.astype(jnp.uint32), jnp.int32)
    ends = jnp.cumsum(padded)
    max_tiles = n_ids // tm + N_EXPERTS
    n_tiles = ends[-1] // tm
    tile_rows = jnp.minimum(jnp.arange(max_tiles + 2, dtype=jnp.int32), n_tiles - 1) * tm
    tile_expert = jnp.minimum(jnp.sum(ends[None, :] <= tile_rows[:, None], axis=1), N_EXPERTS - 1).astype(jnp.int32)
    ys = _experts(a_ext, tile_expert, n_tiles.reshape(1).astype(jnp.int32), h_tiles, layer, w_up, b_up, w_down, b_down)
    return _combine(x, gates, g2, g_final, ys, cond_of_tile, final)


def _rope_tables(n_prompt, batch, length):
    rows = jnp.arange(length) // GRID_W
    cols = jnp.arange(length) % GRID_W
    nfreq = HEAD_DIM // 4
    inv = ROPE_BASE ** (-jnp.arange(nfreq, dtype=F32) / nfreq)
    lane = jnp.arange(LANES)
    is_col = (lane % HEAD_DIM) >= HEAD_DIM // 2
    pos = jnp.where(is_col[None, :], cols[:, None], rows[:, None]).astype(F32)
    ang = pos * inv[lane % nfreq][None, :]
    sign = jnp.where((lane % (2 * nfreq)) < nfreq, -1.0, 1.0)[None, :]
    cos = jnp.tile(jnp.cos(ang), (batch, 1))
    sin = jnp.tile(jnp.sin(ang) * sign, (batch, 1))
    cos = jnp.concatenate([jnp.ones((n_prompt, LANES), F32), cos], axis=0)
    sin = jnp.concatenate([jnp.zeros((n_prompt, LANES), F32), sin], axis=0)
    return cos, sin


def _block_diag_pairs(w):
    w = w.reshape(2, 4, 2, RNN_BLOCK, RNN_BLOCK)
    z = jnp.zeros_like(w[:, :, 0])
    top = jnp.concatenate([w[:, :, 0], z], axis=-1)
    bot = jnp.concatenate([z, w[:, :, 1]], axis=-1)
    return jnp.concatenate([top, bot], axis=-2)


def kernel(x_prompt, x_sample, c, cache_k, cache_v, state_h, c_ctx, w_ada, b_ada, g_norm1, g_norm2, g_final, w_in_ab, sink, w_spatial, b_spatial, g_sgu, w_in_cd, conv_c_w, conv_c_b, w_rg_a, b_rg_a, w_rg_i, b_rg_i, lam, conv_d_w, conv_d_b, ln_d_g, ln_d_b, w_out, w_router, b_router, w_up, b_up, w_down, b_down):
    batch, seq, d = x_prompt.shape
    dec_batch, dec_seq, _ = x_sample.shape
    depth = w_ada.shape[0]
    n_p = batch * seq
    n_s = dec_batch * dec_seq
    past = cache_k.shape[2]
    assert n_p % ROW_TILE == 0 and dec_seq % ROW_TILE == 0 and dec_batch + 1 <= 8
    prompt_tiles = n_p // ROW_TILE
    tiles_per_seq = dec_seq // ROW_TILE

    def cond_of_tile(i):
        return jnp.where(i < prompt_tiles, 0, 1 + (i - prompt_tiles) // tiles_per_seq)

    x = jnp.concatenate([x_prompt.reshape(n_p, d), x_sample.reshape(n_s, d)], axis=0)
    cond8 = jnp.zeros((8, d), F32).at[0].set(c_ctx).at[1:1 + dec_batch].set(c)
    mods = _adaln(cond8, w_ada, b_ada)
    mods = mods.reshape(depth, 8, N_MOD, d).transpose(0, 2, 1, 3)[:, :, :, None, :]
    tables = _rope_tables(n_p, dec_batch, dec_seq)

    new_k, new_v, new_h = [], [], []
    for l in range(depth):
        j = l // 2
        sh1, sc1, g1, sh2, sc2, g2 = (mods[l, m] for m in range(N_MOD))
        if l % 2 == 0:
            w = w_in_ab[j]
            q_end, k_end, v_end = ATT_WIDTH, ATT_WIDTH + KV_WIDTH, ATT_WIDTH + 2 * KV_WIDTH
            w = jnp.concatenate([w[:, :q_end], w[:, v_end:], w[:, q_end:v_end]], axis=1).astype(BF16)
            proj, kv = _inproj(x, g_norm1[l], sc1, sh1, w, cond_of_tile, tables)
            new_k.append(kv[:n_p, :KV_WIDTH].reshape(batch, seq, N_KV_HEADS, HEAD_DIM))
            new_v.append(kv[:n_p, KV_WIDTH:].reshape(batch, seq, N_KV_HEADS, HEAD_DIM))
            sink_b = jnp.broadcast_to(sink[j][:, None], (N_Q_HEADS, LANES))
            att_p = _ctx_attention(proj, sink_b, batch, seq)
            ctx_k = cache_k[:, j].reshape(dec_batch, past, KV_WIDTH).astype(BF16)
            ctx_v = cache_v[:, j].reshape(dec_batch, past, KV_WIDTH).astype(BF16)
            att_s = _lat_attention(proj, sink_b, ctx_k, ctx_v, n_p, dec_batch, dec_seq)
            mix_a = jnp.concatenate([att_p, att_s], axis=0)
            mix_b = _sgu(proj, g_sgu[j], w_spatial[j].astype(BF16), b_spatial[j].T)
        else:
            proj = _inproj(x, g_norm1[l], sc1, sh1, w_in_cd[j].astype(BF16), cond_of_tile)
            wa = _block_diag_pairs(w_rg_a[j])
            wi = _block_diag_pairs(w_rg_i[j])
            w4 = jnp.concatenate([wa[0], wi[0], wa[1], wi[1]], axis=-1).astype(BF16)
            strips = lambda v: v.reshape(RNN_WIDTH // LANES, 1, LANES)
            b4 = jnp.concatenate([strips(b_rg_a[j, 0]), strips(b_rg_i[j, 0]),
                                  strips(b_rg_a[j, 1]), strips(b_rg_i[j, 1])], axis=-1)
            h0_p = jnp.zeros((batch, 2, RNN_WIDTH), F32)
            c_p, h_fin = _rglru(proj, 0, batch, seq, conv_c_w[j], conv_c_b[j], w4, b4, lam[j], h0_p)
            c_s, _ = _rglru(proj, n_p, dec_batch, dec_seq, conv_c_w[j], conv_c_b[j], w4, b4, lam[j], state_h[:, j])
            new_h.append(h_fin)
            w_pad = jnp.concatenate([conv_d_w[j], jnp.zeros((4 * SUBLANES - CONV_K, CONV_WIDTH), F32)], axis=0)
            d_p = _convmod(proj, 0, batch, seq, w_pad, conv_d_b[j], ln_d_g[j], ln_d_b[j])
            d_s = _convmod(proj, n_p, dec_batch, dec_seq, w_pad, conv_d_b[j], ln_d_g[j], ln_d_b[j])
            mix_a = jnp.concatenate([c_p, c_s], axis=0)
            mix_b = jnp.concatenate([d_p, d_s], axis=0)
        w_r = jnp.concatenate([w_router[l], jnp.zeros((d, LANES - N_EXPERTS), F32)], axis=1)
        w_r_hi = w_r.astype(BF16)
        w_r = jnp.concatenate([w_r_hi, (w_r - w_r_hi.astype(F32)).astype(BF16)], axis=1)
        b_r = jnp.concatenate([b_router[l], jnp.full((LANES - N_EXPERTS,), NEG_INF, F32)]).reshape(1, LANES)
        x, h_tiles, logits = _outproj(x, mix_a, mix_b, w_out[l].astype(BF16), g1, g_norm2[l], sc2, sh2, w_r, b_r, cond_of_tile)
        x = _moe(x, h_tiles, logits, g2, g_final, l, w_up, b_up, w_down, b_down, cond_of_tile, final=(l == depth - 1))

    y_prompt = x[:n_p].reshape(batch, seq, d)
    y_sample = x[n_p:].reshape(dec_batch, dec_seq, d)
    return (y_prompt, y_sample, jnp.stack(new_k, axis=1), jnp.stack(new_v, axis=1), jnp.stack(new_h, axis=1))
```

```python
import functools

import jax
import jax.numpy as jnp
from jax import lax
from jax.experimental import pallas as pl
from jax.experimental.pallas import tpu as pltpu

F32 = jnp.float32
BF16 = jnp.bfloat16

HEAD_DIM = 64
N_Q_HEADS = 8
N_KV_HEADS = 2
Q_PER_KV = N_Q_HEADS // N_KV_HEADS
ATT_WIDTH = N_Q_HEADS * HEAD_DIM
KV_WIDTH = N_KV_HEADS * HEAD_DIM
BLOCK = 128
GRID_W = 64
ROPE_BASE = 10000.0
N_SGU_GROUPS = 8
SGU_GROUP = 64
SGU_WIDTH = N_SGU_GROUPS * SGU_GROUP
CHUNK = 128
RNN_BLOCK = 64
RNN_WIDTH = 512
RNN_CONV = 4
RG_C = 8.0
CONV_WIDTH = 512
CONV_K = 31
N_EXPERTS = 32
TOP_K = 4
SWIGLU_LIMIT = 7.0
SWIGLU_ALPHA = 1.702
N_MOD = 6
EPS = 1e-6
NEG_INF = -1e30

LANES = 128
SUBLANES = 8
VMEM_LIMIT = 52 * 1024 * 1024

ROW_TILE = 512
EXPERT_TILE = 256
COMBINE_TILE = 256
COPY_UNROLL = 8
FF_CHUNKS = 4
ID_BITS = 17
COPY_SHARE = (0, 64, 128, 192, 256)
CONV_HALO = 16
SCAN_UNROLL = 4


def _params(*sem):
    return pltpu.CompilerParams(dimension_semantics=sem, vmem_limit_bytes=VMEM_LIMIT)


def _gelu(x):
    return jax.nn.gelu(x, approximate=True)


def _rms_mod(x, g, sc, sh):
    h = x * lax.rsqrt(jnp.mean(x * x, axis=-1, keepdims=True) + EPS) * g
    return h * (1.0 + sc) + sh


def _adaln_kernel(c_ref, w_ref, b_ref, o_ref):
    c = c_ref[...]
    s = c * jax.nn.sigmoid(c)
    o_ref[...] = jnp.dot(s.astype(BF16), w_ref[...].astype(BF16), preferred_element_type=F32) + b_ref[...]


def _adaln(cond8, w_ada, b_ada):
    depth, d, n = w_ada.shape
    tn = n // 4
    return pl.pallas_call(
        _adaln_kernel,
        grid=(depth, n // tn),
        in_specs=[pl.BlockSpec((8, d), lambda l, j: (0, 0)),
                  pl.BlockSpec((None, d, tn), lambda l, j: (l, 0, j)),
                  pl.BlockSpec((None, 1, tn), lambda l, j: (l, 0, j))],
        out_specs=pl.BlockSpec((None, 8, tn), lambda l, j: (l, 0, j)),
        out_shape=jax.ShapeDtypeStruct((depth, 8, n), F32),
        compiler_params=_params("arbitrary", "arbitrary"),
        name="adaln",
    )(cond8, w_ada, b_ada.reshape(depth, 1, n))


def _inproj_kernel(x_ref, g_ref, sc_ref, sh_ref, w_ref, *rest, rope):
    h = _rms_mod(x_ref[...], g_ref[...], sc_ref[...], sh_ref[...])
    acc = jnp.dot(h.astype(BF16), w_ref[...], preferred_element_type=F32)
    if not rope:
        (o_ref,) = rest
        o_ref[...] = acc.astype(BF16)
        return
    cos_ref, sin_ref, o_ref, kv_ref = rest
    kv0 = ATT_WIDTH + 2 * SGU_WIDTH
    kv_ref[...] = acc[:, kv0:kv0 + 2 * KV_WIDTH]
    cos = cos_ref[...]
    sin = sin_ref[...]
    lane = lax.broadcasted_iota(jnp.int32, cos.shape, 1)
    first = (lane % 32) < 16
    o_ref[...] = acc.astype(BF16)
    for j in (0, 1, 2, 3, kv0 // LANES):
        seg = acc[:, LANES * j:LANES * (j + 1)]
        partner = jnp.where(first, pltpu.roll(seg, LANES - 16, 1), pltpu.roll(seg, 16, 1))
        o_ref[:, LANES * j:LANES * (j + 1)] = (seg * cos + partner * sin).astype(BF16)


def _inproj(x, g, sc, sh, w, cond_of_tile, tables=None):
    n, d = x.shape
    nout = w.shape[1]
    tm = ROW_TILE
    rope = tables is not None
    in_specs = [pl.BlockSpec((tm, d), lambda i: (i, 0)),
                pl.BlockSpec((1, d), lambda i: (0, 0)),
                pl.BlockSpec((None, 1, d), lambda i: (cond_of_tile(i), 0, 0)),
                pl.BlockSpec((None, 1, d), lambda i: (cond_of_tile(i), 0, 0)),
                pl.BlockSpec((d, nout), lambda i: (0, 0))]
    args = [x, g.reshape(1, d), sc, sh, w]
    out_specs = pl.BlockSpec((tm, nout), lambda i: (i, 0))
    out_shape = jax.ShapeDtypeStruct((n, nout), BF16)
    if rope:
        in_specs += [pl.BlockSpec((tm, LANES), lambda i: (i, 0))] * 2
        args += list(tables)
        out_specs = [out_specs, pl.BlockSpec((tm, 2 * KV_WIDTH), lambda i: (i, 0))]
        out_shape = [out_shape, jax.ShapeDtypeStruct((n, 2 * KV_WIDTH), F32)]
    return pl.pallas_call(
        functools.partial(_inproj_kernel, rope=rope),
        grid=(n // tm,),
        in_specs=in_specs, out_specs=out_specs, out_shape=out_shape,
        compiler_params=_params("arbitrary"),
        name="inproj_rope" if rope else "inproj",
    )(*args)


def _attend(qh, keys, values, masks, sink):
    scale = HEAD_DIM ** -0.5
    scores = []
    for kk, mk in zip(keys, masks):
        s = lax.dot_general(qh, kk, (((1,), (1,)), ((), ())), preferred_element_type=F32) * scale
        if mk is not None:
            s = jnp.where(mk, s, NEG_INF)
        scores.append(s)
    m = sink
    for s in scores:
        m = jnp.maximum(m, jnp.max(s, axis=-1, keepdims=True))
    es = [jnp.exp(s - m) for s in scores]
    den = jnp.exp(sink - m)
    for e in es:
        den = den + jnp.sum(e, axis=-1, keepdims=True)
    inv = 1.0 / den
    out = None
    for e, vv in zip(es, values):
        o = jnp.dot((e * inv).astype(BF16), vv, preferred_element_type=F32)
        out = o if out is None else out + o
    return out


def _ctx_attn_kernel(sink_ref, q_ref, kv_ref, o_ref):
    q = q_ref[...]
    kv = kv_ref[...]
    for h in range(N_Q_HEADS):
        g = h // Q_PER_KV
        qh = q[:, HEAD_DIM * h:HEAD_DIM * (h + 1)]
        kg = kv[:, HEAD_DIM * g:HEAD_DIM * (g + 1)]
        vg = kv[:, KV_WIDTH + HEAD_DIM * g:KV_WIDTH + HEAD_DIM * (g + 1)]
        out = _attend(qh, [kg], [vg], [None], sink_ref[h:h + 1, 0:1])
        o_ref[:, HEAD_DIM * h:HEAD_DIM * (h + 1)] = out.astype(BF16)


def _ctx_attention(proj, sink_b, batch, seq):
    kv_blk = (ATT_WIDTH + 2 * SGU_WIDTH) // (2 * KV_WIDTH)
    return pl.pallas_call(
        _ctx_attn_kernel,
        grid=(batch,),
        in_specs=[pl.BlockSpec((N_Q_HEADS, LANES), lambda b: (0, 0)),
                  pl.BlockSpec((seq, ATT_WIDTH), lambda b: (b, 0)),
                  pl.BlockSpec((seq, 2 * KV_WIDTH), lambda b: (b, kv_blk))],
        out_specs=pl.BlockSpec((seq, ATT_WIDTH), lambda b: (b, 0)),
        out_shape=jax.ShapeDtypeStruct((batch * seq, ATT_WIDTH), BF16),
        compiler_params=_params("arbitrary"),
        name="ctx_attention",
    )(sink_b, proj, proj)


def _lat_attn_kernel(sink_ref, q_ref, kvp_ref, kvc_ref, kvn_ref, ck_ref, cv_ref, o_ref):
    n = pl.program_id(1)
    nb = pl.num_programs(1)
    q = q_ref[...]
    kvw = jnp.concatenate([kvp_ref[...], kvc_ref[...], kvn_ref[...]], axis=0)
    ck = ck_ref[...]
    cv = cv_ref[...]
    qi = lax.broadcasted_iota(jnp.int32, (BLOCK, 3 * BLOCK), 0)
    kj = lax.broadcasted_iota(jnp.int32, (BLOCK, 3 * BLOCK), 1)
    lo = jnp.where(n > 0, 0, BLOCK)
    hi = jnp.where(n < nb - 1, 3 * BLOCK, 2 * BLOCK)
    valid = (kj >= qi) & (kj <= qi + 2 * BLOCK) & (kj >= lo) & (kj < hi)
    for h in range(N_Q_HEADS):
        g = h // Q_PER_KV
        qh = q[:, HEAD_DIM * h:HEAD_DIM * (h + 1)]
        kw = kvw[:, HEAD_DIM * g:HEAD_DIM * (g + 1)]
        vw = kvw[:, KV_WIDTH + HEAD_DIM * g:KV_WIDTH + HEAD_DIM * (g + 1)]
        kc = ck[:, HEAD_DIM * g:HEAD_DIM * (g + 1)]
        vc = cv[:, HEAD_DIM * g:HEAD_DIM * (g + 1)]
        out = _attend(qh, [kw, kc], [vw, vc], [valid, None], sink_ref[h:h + 1, 0:1])
        o_ref[:, HEAD_DIM * h:HEAD_DIM * (h + 1)] = out.astype(BF16)


def _lat_attention(proj, sink_b, ctx_k, ctx_v, row0, batch, length):
    nb = length // BLOCK
    b0 = row0 // BLOCK
    kv_blk = (ATT_WIDTH + 2 * SGU_WIDTH) // (2 * KV_WIDTH)
    past = ctx_k.shape[1]
    kv_spec = lambda f: pl.BlockSpec((BLOCK, 2 * KV_WIDTH), lambda b, n: (b0 + b * nb + f(n), kv_blk))
    return pl.pallas_call(
        _lat_attn_kernel,
        grid=(batch, nb),
        in_specs=[pl.BlockSpec((N_Q_HEADS, LANES), lambda b, n: (0, 0)),
                  pl.BlockSpec((BLOCK, ATT_WIDTH), lambda b, n: (b0 + b * nb + n, 0)),
                  kv_spec(lambda n: jnp.maximum(n - 1, 0)),
                  kv_spec(lambda n: n),
                  kv_spec(lambda n: jnp.minimum(n + 1, nb - 1)),
                  pl.BlockSpec((None, past, KV_WIDTH), lambda b, n: (b, 0, 0)),
                  pl.BlockSpec((None, past, KV_WIDTH), lambda b, n: (b, 0, 0))],
        out_specs=pl.BlockSpec((BLOCK, ATT_WIDTH), lambda b, n: (b * nb + n, 0)),
        out_shape=jax.ShapeDtypeStruct((batch * length, ATT_WIDTH), BF16),
        compiler_params=_params("arbitrary", "arbitrary"),
        name="latent_attention",
    )(sink_b, proj, proj, proj, proj, ctx_k, ctx_v)


def _sgu_kernel(u_ref, v_ref, g_ref, ws_ref, bs_ref, o_ref):
    rows = u_ref.shape[0]
    v = _gelu(v_ref[...].astype(F32))
    mu = jnp.mean(v, axis=-1, keepdims=True)
    vc = v - mu
    v = vc * lax.rsqrt(jnp.mean(vc * vc, axis=-1, keepdims=True) + EPS) * g_ref[...]
    vb = v.astype(BF16)
    for c in range(rows // CHUNK):
        r0 = c * CHUNK
        for g in range(N_SGU_GROUPS):
            c0 = g * SGU_GROUP
            mixed = jnp.dot(ws_ref[g], vb[r0:r0 + CHUNK, c0:c0 + SGU_GROUP], preferred_element_type=F32)
            mixed = mixed + bs_ref[:, g:g + 1]
            u = _gelu(u_ref[r0:r0 + CHUNK, c0:c0 + SGU_GROUP].astype(F32))
            o_ref[r0:r0 + CHUNK, c0:c0 + SGU_GROUP] = (u * mixed).astype(BF16)


def _sgu(proj, g_sgu, w_s, b_s_t):
    n = proj.shape[0]
    tm = 2 * CHUNK
    return pl.pallas_call(
        _sgu_kernel,
        grid=(n // tm,),
        in_specs=[pl.BlockSpec((tm, SGU_WIDTH), lambda i: (i, 1)),
                  pl.BlockSpec((tm, SGU_WIDTH), lambda i: (i, 2)),
                  pl.BlockSpec((1, SGU_WIDTH), lambda i: (0, 0)),
                  pl.BlockSpec((N_SGU_GROUPS, CHUNK, CHUNK), lambda i: (0, 0, 0)),
                  pl.BlockSpec((CHUNK, N_SGU_GROUPS), lambda i: (0, 0))],
        out_specs=pl.BlockSpec((tm, SGU_WIDTH), lambda i: (i, 0)),
        out_shape=jax.ShapeDtypeStruct((n, SGU_WIDTH), BF16),
        compiler_params=_params("arbitrary"),
        name="sgu",
    )(proj, proj, g_sgu.reshape(1, SGU_WIDTH), w_s, b_s_t)


def _scan8(a, b, reverse):
    row = lax.broadcasted_iota(jnp.int32, a.shape, 0)
    for d in (1, 2, 4):
        if reverse:
            keep = row < SUBLANES - d
            shift = SUBLANES - d
        else:
            keep = row >= d
            shift = d
        a_sh = jnp.where(keep, pltpu.roll(a, shift, 0), 1.0)
        b_sh = jnp.where(keep, pltpu.roll(b, shift, 0), 0.0)
        b = b + a * b_sh
        a = a * a_sh
    return a, b


def _rglru_kernel(gate_ref, xr_ref, cw_ref, cb_ref, w4_ref, b4_ref, lam_ref, h0_ref, o_ref, hl_ref,
                  xpad, a_f, b_f, a_b, b_b, y_f, y_b):
    length = xr_ref.shape[0]
    xpad[0:SUBLANES, :] = jnp.zeros((SUBLANES, LANES), F32)
    xpad[SUBLANES:SUBLANES + length, :] = xr_ref[...].astype(F32)
    xpad[SUBLANES + length:, :] = jnp.zeros((SUBLANES, LANES), F32)
    left = RNN_CONV // 2
    xc = cb_ref[...] + cw_ref[0:1, :] * xpad[SUBLANES - left:SUBLANES - left + length, :]
    for j in range(1, RNN_CONV):
        xc = xc + cw_ref[j:j + 1, :] * xpad[SUBLANES - left + j:SUBLANES - left + j + length, :]
    pre = jnp.dot(xc.astype(BF16), w4_ref[...], preferred_element_type=F32) + b4_ref[...]
    lam = lam_ref[...]
    sp = jnp.maximum(-lam, 0.0) + jnp.log(1.0 + jnp.exp(-jnp.abs(lam)))
    for d, (a_ref, b_ref) in enumerate(((a_f, b_f), (a_b, b_b))):
        r = jax.nn.sigmoid(pre[:, 2 * d * LANES:(2 * d + 1) * LANES])
        i = jax.nn.sigmoid(pre[:, (2 * d + 1) * LANES:(2 * d + 2) * LANES])
        log_a = -RG_C * r * sp[d:d + 1, :]
        a = jnp.exp(log_a)
        a_ref[...] = a
        b_ref[...] = jnp.sqrt(1.0 - a * a) * (i * xc)

    groups = length // SUBLANES

    def body(it, carry):
        hf, hb = carry
        for v in range(SCAN_UNROLL):
            g = it * SCAN_UNROLL + v
            rf = pl.multiple_of(g * SUBLANES, SUBLANES)
            rb = pl.multiple_of((groups - 1 - g) * SUBLANES, SUBLANES)
            af, bf = _scan8(a_f[pl.ds(rf, SUBLANES), :], b_f[pl.ds(rf, SUBLANES), :], False)
            ab, bb = _scan8(a_b[pl.ds(rb, SUBLANES), :], b_b[pl.ds(rb, SUBLANES), :], True)
            yf = bf + af * hf
            yb = bb + ab * hb
            y_f[pl.ds(rf, SUBLANES), :] = yf
            y_b[pl.ds(rb, SUBLANES), :] = yb
            hf, hb = yf[SUBLANES - 1:SUBLANES, :], yb[0:1, :]
        return hf, hb

    hf, hb = lax.fori_loop(0, groups // SCAN_UNROLL, body, (h0_ref[0:1, :], h0_ref[1:2, :]))
    hl_ref[0:1, :] = hf
    hl_ref[1:2, :] = hb
    o_ref[...] = (_gelu(gate_ref[...].astype(F32)) * (y_f[...] + y_b[...])).astype(BF16)


def _rglru(proj, row0, batch, length, conv_w, conv_b, w4, b4, lam, h0):
    strips = RNN_WIDTH // LANES
    r0 = row0 // length
    seq = lambda c0: pl.BlockSpec((length, LANES), lambda b, j: (r0 + b, c0 + j))
    strip = lambda rows: pl.BlockSpec((rows, LANES), lambda b, j: (0, j))
    scratch = [pltpu.VMEM((length + 2 * SUBLANES, LANES), F32)] + [pltpu.VMEM((length, LANES), F32)] * 6
    return pl.pallas_call(
        _rglru_kernel,
        grid=(batch, strips),
        in_specs=[seq(0), seq(strips), strip(RNN_CONV), strip(1),
                  pl.BlockSpec((None, LANES, 4 * LANES), lambda b, j: (j, 0, 0)),
                  pl.BlockSpec((None, 1, 4 * LANES), lambda b, j: (j, 0, 0)),
                  strip(2),
                  pl.BlockSpec((None, 2, LANES), lambda b, j: (b, 0, j))],
        out_specs=[pl.BlockSpec((length, LANES), lambda b, j: (b, j)),
                   pl.BlockSpec((None, 2, LANES), lambda b, j: (b, 0, j))],
        out_shape=[jax.ShapeDtypeStruct((batch * length, RNN_WIDTH), BF16),
                   jax.ShapeDtypeStruct((batch, 2, RNN_WIDTH), F32)],
        scratch_shapes=scratch,
        compiler_params=_params("arbitrary", "arbitrary"),
        name="rglru",
    )(proj, proj, conv_w, conv_b.reshape(1, RNN_WIDTH), w4, b4, lam, h0)


def _convmod_kernel(a_ref, b_ref, pa_ref, pb_ref, na_ref, nb_ref, w_ref, cb_ref, g_ref, beta_ref, o_ref, zpad):
    t = pl.program_id(1)
    nt = pl.num_programs(1)
    rows = a_ref.shape[0]

    def glu(x_ref, y_ref):
        return x_ref[...].astype(F32) * jax.nn.sigmoid(y_ref[...].astype(F32))

    zpad[0:CONV_HALO, :] = glu(pa_ref, pb_ref) * (t > 0).astype(F32)
    zpad[CONV_HALO:CONV_HALO + rows, :] = glu(a_ref, b_ref)
    zpad[CONV_HALO + rows:, :] = glu(na_ref, nb_ref) * (t < nt - 1).astype(F32)
    off = CONV_HALO - CONV_K // 2
    acc = cb_ref[...] + w_ref[0:1, :] * zpad[off:off + rows, :]
    for j in range(1, CONV_K):
        acc = acc + w_ref[j:j + 1, :] * zpad[off + j:off + j + rows, :]
    mu = jnp.mean(acc, axis=-1, keepdims=True)
    xc = acc - mu
    y = xc * lax.rsqrt(jnp.mean(xc * xc, axis=-1, keepdims=True) + EPS) * g_ref[...] + beta_ref[...]
    o_ref[...] = (y * jax.nn.sigmoid(y)).astype(BF16)


def _convmod(proj, row0, batch, length, w_pad, cb, ln_g, ln_b):
    tl = min(length, ROW_TILE)
    nt = length // tl
    r0 = row0 // tl
    hb = tl // CONV_HALO
    h0 = row0 // CONV_HALO
    nh = (batch * length + row0) // CONV_HALO
    ca, cbk = 2, 3
    main = lambda c: pl.BlockSpec((tl, CONV_WIDTH), lambda b, t: (r0 + b * nt + t, c))
    prev = lambda c: pl.BlockSpec((CONV_HALO, CONV_WIDTH),
                                  lambda b, t: (jnp.maximum(h0 + (b * nt + t) * hb - 1, 0), c))
    nxt = lambda c: pl.BlockSpec((CONV_HALO, CONV_WIDTH),
                                 lambda b, t: (jnp.minimum(h0 + (b * nt + t + 1) * hb, nh - 1), c))
    vec = pl.BlockSpec((1, CONV_WIDTH), lambda b, t: (0, 0))
    return pl.pallas_call(
        _convmod_kernel,
        grid=(batch, nt),
        in_specs=[main(ca), main(cbk), prev(ca), prev(cbk), nxt(ca), nxt(cbk),
                  pl.BlockSpec((4 * SUBLANES, CONV_WIDTH), lambda b, t: (0, 0)), vec, vec, vec],
        out_specs=pl.BlockSpec((tl, CONV_WIDTH), lambda b, t: (b * nt + t, 0)),
        out_shape=jax.ShapeDtypeStruct((batch * length, CONV_WIDTH), BF16),
        scratch_shapes=[pltpu.VMEM((tl + 2 * CONV_HALO, CONV_WIDTH), F32)],
        compiler_params=_params("arbitrary", "arbitrary"),
        name="convmod",
    )(proj, proj, proj, proj, proj, proj, w_pad, cb.reshape(1, -1), ln_g.reshape(1, -1), ln_b.reshape(1, -1))


def _store_row_tiles(ref, val):
    rows, d = val.shape
    seg = d // LANES
    for s in range(seg):
        ref[pl.ds(s, rows, stride=seg), :] = val[:, LANES * s:LANES * (s + 1)]


def _load_row_tiles(ref, rows, seg, base=0):
    return jnp.concatenate([ref[pl.ds(base + s, rows, stride=seg), :] for s in range(seg)], axis=-1)


def _outproj_kernel(x_ref, ma_ref, mb_ref, wo_ref, g1_ref, gn_ref, sc_ref, sh_ref, wr_ref, br_ref,
                    xo_ref, h_ref, lg_ref):
    half = ma_ref.shape[1]
    y = jnp.dot(ma_ref[...], wo_ref[0:half, :], preferred_element_type=F32)
    y = y + jnp.dot(mb_ref[...], wo_ref[half:, :], preferred_element_type=F32)
    x = x_ref[...] + g1_ref[...] * y
    xo_ref[...] = x
    h = _rms_mod(x, gn_ref[...], sc_ref[...], sh_ref[...])
    _store_row_tiles(h_ref, h)
    h_hi = h.astype(BF16)
    h_lo = (h - h_hi.astype(F32)).astype(BF16)
    r = jnp.dot(h_hi, wr_ref[...], preferred_element_type=F32)
    r = r[:, :LANES] + r[:, LANES:] + jnp.dot(h_lo, wr_ref[:, :LANES], preferred_element_type=F32)
    lg_ref[...] = r + br_ref[...]


def _outproj(x, mix_a, mix_b, w_out, g1, gn, sc, sh, w_r, b_r, cond_of_tile):
    n, d = x.shape
    tm = ROW_TILE
    seg = d // LANES
    half = mix_a.shape[1]
    row = lambda w: pl.BlockSpec((tm, w), lambda i: (i, 0))
    const = lambda r, c: pl.BlockSpec((r, c), lambda i: (0, 0))
    mod = pl.BlockSpec((None, 1, d), lambda i: (cond_of_tile(i), 0, 0))
    return pl.pallas_call(
        _outproj_kernel,
        grid=(n // tm,),
        in_specs=[row(d), row(half), row(half), const(2 * half, d), mod, const(1, d), mod, mod,
                  const(d, 2 * LANES), const(1, LANES)],
        out_specs=[row(d), pl.BlockSpec((tm * seg, LANES), lambda i: (i, 0)), row(LANES)],
        out_shape=[jax.ShapeDtypeStruct((n, d), F32), jax.ShapeDtypeStruct((n * seg, LANES), F32),
                   jax.ShapeDtypeStruct((n, LANES), F32)],
        compiler_params=_params("arbitrary"),
        name="outproj",
    )(x, mix_a, mix_b, w_out, g1, gn.reshape(1, d), sc, sh, w_r, b_r)


def _route_kernel(lg_ref, idx_ref, gate_ref, cnt_ref, carry):
    @pl.when(pl.program_id(0) == 0)
    def _():
        carry[...] = jnp.zeros_like(carry)

    lg = lg_ref[...]
    lane = lax.broadcasted_iota(jnp.int32, lg.shape, 1).astype(F32)
    work = lg
    vals, idxs = [], []
    for _ in range(TOP_K):
        m = jnp.max(work, axis=-1, keepdims=True)
        ik = jnp.min(jnp.where(work == m, lane, float(LANES)), axis=-1, keepdims=True)
        vals.append(m)
        idxs.append(ik)
        work = jnp.where(lane == ik, -jnp.inf, work)
    es = [jnp.exp(v - vals[0]) for v in vals]
    den = es[0] + es[1] + es[2] + es[3]
    onehot = jnp.zeros(lg.shape, F32)
    for k in range(TOP_K):
        idx_ref[:, k:k + 1] = idxs[k].astype(jnp.int32)
        gate_ref[:, k:k + 1] = es[k] / den
        onehot = onehot + (lane == idxs[k]).astype(F32)
    carry[...] = carry[...] + jnp.sum(onehot, axis=0, keepdims=True)
    cnt_ref[...] = carry[...]


def _route(logits):
    n = logits.shape[0]
    tm = ROW_TILE
    small = lambda dt: jax.ShapeDtypeStruct((n, TOP_K), dt)
    return pl.pallas_call(
        _route_kernel,
        grid=(n // tm,),
        in_specs=[pl.BlockSpec((tm, LANES), lambda i: (i, 0))],
        out_specs=[pl.BlockSpec((tm, TOP_K), lambda i: (i, 0))] * 2 + [pl.BlockSpec((1, LANES), lambda i: (0, 0))],
        out_shape=[small(jnp.int32), small(F32), jax.ShapeDtypeStruct((1, LANES), F32)],
        scratch_shapes=[pltpu.VMEM((1, LANES), F32)],
        compiler_params=_params("arbitrary"),
        name="route",
    )(logits)


def _row_tile(ref, r):
    return ref.at[pl.ds(pl.multiple_of(r * SUBLANES, SUBLANES), SUBLANES)]


def _expert_kernel(a_ref, te_ref, nt_ref, h_ref, wu_ref, bu_ref, wd_ref, bd_ref, ys_ref,
                   xbuf0, xbuf1, ybuf0, ybuf1, xb, yacc, wu_bf, wd_bf, sem_g, sem_s):
    j = pl.program_id(0)
    tm = EXPERT_TILE
    n_tok = h_ref.shape[0] // SUBLANES
    n_tiles = nt_ref[0]
    xbufs, ybufs = (xbuf0, xbuf1), (ybuf0, ybuf1)
    assert len(COPY_SHARE) == FF_CHUNKS + 1 and COPY_SHARE[-1] == tm
    cw = wd_bf.shape[1]

    def gather(tile, s, i):
        tok = lax.shift_right_logical(a_ref[(tile + 1) * tm + i], ID_BITS)
        return pltpu.make_async_copy(_row_tile(h_ref, tok), _row_tile(xbufs[s], i), sem_g.at[s])

    def scatter(tile, s, i):
        row = a_ref[(tile + 1) * tm + i] & ((1 << ID_BITS) - 1)
        return pltpu.make_async_copy(_row_tile(ybufs[s], i), _row_tile(ys_ref, row), sem_s.at[s])

    def start_scatter(tile, s, i):
        row = a_ref[(tile + 1) * tm + i] & ((1 << ID_BITS) - 1)
        pltpu.async_copy(_row_tile(ybufs[s], i), _row_tile(ys_ref, row), sem_s.at[s], priority=1)

    def in_loop(make, act):
        def body(g, carry):
            for u in range(COPY_UNROLL):
                act(make(g * COPY_UNROLL + u))
            return carry
        lax.fori_loop(0, tm // COPY_UNROLL, body, 0)

    @pl.when(j == 0)
    def _():
        ybuf1[...] = jnp.zeros_like(ybuf1)
        in_loop(lambda i: gather(0, 0, i), lambda c: c.start())

    changed = jnp.logical_or(j == 0, te_ref[j] != te_ref[jnp.maximum(j - 1, 0)])

    @pl.when(jnp.logical_and(j <= n_tiles, changed))
    def _():
        for c in range(FF_CHUNKS):
            wu_bf[c] = wu_ref[:, c * cw:(c + 1) * cw].astype(BF16)
            wu_bf[FF_CHUNKS + c] = wu_ref[:, (FF_CHUNKS + c) * cw:(FF_CHUNKS + c + 1) * cw].astype(BF16)
            wd_bf[c] = wd_ref[c * cw:(c + 1) * cw, :].astype(BF16)

    for s in range(2):
        @pl.when(jnp.logical_and(j <= n_tiles + 1, j % 2 == s))
        def _():
            in_loop(lambda i: gather(j, s, i), lambda c: c.wait())

            @pl.when(j >= 1)
            def _():
                in_loop(lambda i: scatter(j - 2, s, i), lambda c: c.wait())

        @pl.when(jnp.logical_and(j <= n_tiles, j % 2 == s))
        def _():
            xb[...] = _load_row_tiles(xbufs[s], tm, SUBLANES).astype(BF16)

        for c in range(FF_CHUNKS):
            @pl.when(jnp.logical_and(jnp.logical_and(j <= n_tiles, j % 2 == s), te_ref[c] >= 0))
            def _():
                for i in range(COPY_SHARE[c], COPY_SHARE[c + 1]):
                    gather(j + 1, 1 - s, i).start()
                    start_scatter(j - 1, 1 - s, i)
                x = xb[...]
                g = jnp.dot(x, wu_bf[c], preferred_element_type=F32) + bu_ref[c]
                u = jnp.dot(x, wu_bf[FF_CHUNKS + c], preferred_element_type=F32) + bu_ref[FF_CHUNKS + c]
                g = jnp.minimum(g, SWIGLU_LIMIT)
                u = jnp.clip(u, -SWIGLU_LIMIT, SWIGLU_LIMIT)
                act = (u + 1.0) * (g * jax.nn.sigmoid(SWIGLU_ALPHA * g))
                part = jnp.dot(act.astype(BF16), wd_bf[c], preferred_element_type=F32)
                if c == 0:
                    yacc[...] = part + bd_ref[...]
                elif c < FF_CHUNKS - 1:
                    yacc[...] += part
                else:
                    _store_row_tiles(ybufs[s], yacc[...] + part)


def _experts(a_ext, tile_expert, n_tiles, h_tiles, layer, w_up, b_up, w_down, b_down):
    depth, n_exp, d, ff2 = w_up.shape
    ff = w_down.shape[2]
    tm = EXPERT_TILE
    n_tok = h_tiles.shape[0] // SUBLANES
    pick = lambda j, a, te, nt: (layer, te[j], 0, 0)
    cw = ff // FF_CHUNKS
    tile_buf = pltpu.VMEM((tm * SUBLANES, LANES), F32)
    grid_spec = pltpu.PrefetchScalarGridSpec(
        num_scalar_prefetch=3, grid=(tile_expert.shape[0],),
        in_specs=[pl.BlockSpec(memory_space=pl.ANY),
                  pl.BlockSpec((None, None, d, ff2), pick),
                  pl.BlockSpec((None, None, 2 * FF_CHUNKS, 1, cw), lambda j, a, te, nt: (layer, te[j], 0, 0, 0)),
                  pl.BlockSpec((None, None, ff, d), pick),
                  pl.BlockSpec((None, None, 1, d), pick)],
        out_specs=pl.BlockSpec(memory_space=pl.ANY),
        scratch_shapes=[tile_buf, tile_buf, tile_buf, tile_buf,
                        pltpu.VMEM((tm, d), BF16), pltpu.VMEM((tm, d), F32),
                        pltpu.VMEM((2 * FF_CHUNKS, d, cw), BF16), pltpu.VMEM((FF_CHUNKS, cw, d), BF16),
                        pltpu.SemaphoreType.DMA((2,)), pltpu.SemaphoreType.DMA((2,))])
    return pl.pallas_call(
        _expert_kernel,
        grid_spec=grid_spec,
        out_shape=jax.ShapeDtypeStruct(((TOP_K * n_tok + tm) * SUBLANES, LANES), F32),
        compiler_params=pltpu.CompilerParams(dimension_semantics=("arbitrary",), has_side_effects=True,
                                             vmem_limit_bytes=VMEM_LIMIT),
        name="experts",
    )(a_ext, tile_expert, n_tiles, h_tiles,
      w_up, b_up.reshape(depth, n_exp, 2 * FF_CHUNKS, 1, cw), w_down, b_down.reshape(depth, n_exp, 1, d))


def _combine_kernel(x_ref, gate_ref, g2_ref, gf_ref, y_ref, o_ref, *, final):
    tc = x_ref.shape[0]
    gates = gate_ref[...]
    acc = gates[:, 0:1] * _load_row_tiles(y_ref, tc, SUBLANES)
    for k in range(1, TOP_K):
        acc = acc + gates[:, k:k + 1] * _load_row_tiles(y_ref, tc, SUBLANES, base=k * tc * SUBLANES)
    x = x_ref[...] + g2_ref[...] * acc
    if final:
        x = x * lax.rsqrt(jnp.mean(x * x, axis=-1, keepdims=True) + EPS) * gf_ref[...]
    o_ref[...] = x


def _combine(x, gates, g2, g_final, ys, cond_of_tile, final):
    n, d = x.shape
    tc = COMBINE_TILE
    per_row_tile = ROW_TILE // tc
    return pl.pallas_call(
        functools.partial(_combine_kernel, final=final),
        grid=(n // tc,),
        in_specs=[pl.BlockSpec((tc, d), lambda i: (i, 0)),
                  pl.BlockSpec((tc, TOP_K), lambda i: (i, 0)),
                  pl.BlockSpec((None, 1, d), lambda i: (cond_of_tile(i // per_row_tile), 0, 0)),
                  pl.BlockSpec((1, d), lambda i: (0, 0)),
                  pl.BlockSpec((tc * TOP_K * SUBLANES, LANES), lambda i: (i, 0))],
        out_specs=pl.BlockSpec((tc, d), lambda i: (i, 0)),
        out_shape=jax.ShapeDtypeStruct((n, d), F32),
        compiler_params=_params("arbitrary"),
        name="combine_final" if final else "combine",
    )(x, gates, g2, g_final.reshape(1, d), ys)


def _moe(x, h_tiles, logits, g2, g_final, layer, w_up, b_up, w_down, b_down, cond_of_tile, final):
    n = x.shape[0]
    tm = EXPERT_TILE
    idx, gates, counts = _route(logits)
    n_ids = n * TOP_K
    id_bits = ID_BITS
    assert n_ids + tm <= (1 << ID_BITS) and n <= (1 << (32 - ID_BITS))
    tok = jnp.arange(n, dtype=jnp.int32)[:, None]
    out_row = ((tok // COMBINE_TILE) * (TOP_K * COMBINE_TILE) + jnp.arange(TOP_K, dtype=jnp.int32)[None, :] * COMBINE_TILE
               + tok % COMBINE_TILE)
    counts = counts[0, :N_EXPERTS].astype(jnp.int32)
    padded = ((counts + tm - 1) // tm) * tm
    pad_c = jnp.arange(tm, dtype=jnp.int32)
    pad_expert = jnp.where(pad_c[None, :] < (padded - counts)[:, None],
                           jnp.arange(N_EXPERTS, dtype=jnp.int32)[:, None], N_EXPERTS)
    keys = jnp.concatenate([(idx * (1 << id_bits) + out_row).reshape(-1),
                            (pad_expert * (1 << id_bits) + (n_ids + pad_c)[None, :]).reshape(-1)])
    rows = jnp.sort(keys) & ((1 << id_bits) - 1)
    rows = jnp.concatenate([n_ids + pad_c, rows, n_ids + pad_c, n_ids + pad_c])
    blk = TOP_K * COMBINE_TILE
    src = jnp.minimum((rows // blk) * COMBINE_TILE + rows % COMBINE_TILE, n - 1)
    a_ext = lax.bitcast_convert_type((src.astype(jnp.uint32) << ID_BITS) | rows.astype(jnp.uint32), jnp.int32)
    ends = jnp.cumsum(padded)
    max_tiles = n_ids // tm + N_EXPERTS
    n_tiles = ends[-1] // tm
    tile_rows = jnp.minimum(jnp.arange(max_tiles + 2, dtype=jnp.int32), n_tiles - 1) * tm
    tile_expert = jnp.minimum(jnp.sum(ends[None, :] <= tile_rows[:, None], axis=1), N_EXPERTS - 1).astype(jnp.int32)
    ys = _experts(a_ext, tile_expert, n_tiles.reshape(1).astype(jnp.int32), h_tiles, layer, w_up, b_up, w_down, b_down)
    return _combine(x, gates, g2, g_final, ys, cond_of_tile, final)


def _rope_tables(n_prompt, batch, length):
    rows = jnp.arange(length) // GRID_W
    cols = jnp.arange(length) % GRID_W
    nfreq = HEAD_DIM // 4
    inv = ROPE_BASE ** (-jnp.arange(nfreq, dtype=F32) / nfreq)
    lane = jnp.arange(LANES)
    is_col = (lane % HEAD_DIM) >= HEAD_DIM // 2
    pos = jnp.where(is_col[None, :], cols[:, None], rows[:, None]).astype(F32)
    ang = pos * inv[lane % nfreq][None, :]
    sign = jnp.where((lane % (2 * nfreq)) < nfreq, -1.0, 1.0)[None, :]
    cos = jnp.tile(jnp.cos(ang), (batch, 1))
    sin = jnp.tile(jnp.sin(ang) * sign, (batch, 1))
    cos = jnp.concatenate([jnp.ones((n_prompt, LANES), F32), cos], axis=0)
    sin = jnp.concatenate([jnp.zeros((n_prompt, LANES), F32), sin], axis=0)
    return cos, sin


def _block_diag_pairs(w):
    w = w.reshape(2, 4, 2, RNN_BLOCK, RNN_BLOCK)
    z = jnp.zeros_like(w[:, :, 0])
    top = jnp.concatenate([w[:, :, 0], z], axis=-1)
    bot = jnp.concatenate([z, w[:, :, 1]], axis=-1)
    return jnp.concatenate([top, bot], axis=-2)


def kernel(x_prompt, x_sample, c, cache_k, cache_v, state_h, c_ctx, w_ada, b_ada, g_norm1, g_norm2, g_final, w_in_ab, sink, w_spatial, b_spatial, g_sgu, w_in_cd, conv_c_w, conv_c_b, w_rg_a, b_rg_a, w_rg_i, b_rg_i, lam, conv_d_w, conv_d_b, ln_d_g, ln_d_b, w_out, w_router, b_router, w_up, b_up, w_down, b_down):
    batch, seq, d = x_prompt.shape
    dec_batch, dec_seq, _ = x_sample.shape
    depth = w_ada.shape[0]
    n_p = batch * seq
    n_s = dec_batch * dec_seq
    past = cache_k.shape[2]
    assert n_p % ROW_TILE == 0 and dec_seq % ROW_TILE == 0 and dec_batch + 1 <= 8
    prompt_tiles = n_p // ROW_TILE
    tiles_per_seq = dec_seq // ROW_TILE

    def cond_of_tile(i):
        return jnp.where(i < prompt_tiles, 0, 1 + (i - prompt_tiles) // tiles_per_seq)

    x = jnp.concatenate([x_prompt.reshape(n_p, d), x_sample.reshape(n_s, d)], axis=0)
    cond8 = jnp.zeros((8, d), F32).at[0].set(c_ctx).at[1:1 + dec_batch].set(c)
    mods = _adaln(cond8, w_ada, b_ada)
    mods = mods.reshape(depth, 8, N_MOD, d).transpose(0, 2, 1, 3)[:, :, :, None, :]
    tables = _rope_tables(n_p, dec_batch, dec_seq)

    new_k, new_v, new_h = [], [], []
    for l in range(depth):
        j = l // 2
        sh1, sc1, g1, sh2, sc2, g2 = (mods[l, m] for m in range(N_MOD))
        if l % 2 == 0:
            w = w_in_ab[j]
            q_end, k_end, v_end = ATT_WIDTH, ATT_WIDTH + KV_WIDTH, ATT_WIDTH + 2 * KV_WIDTH
            w = jnp.concatenate([w[:, :q_end], w[:, v_end:], w[:, q_end:v_end]], axis=1).astype(BF16)
            proj, kv = _inproj(x, g_norm1[l], sc1, sh1, w, cond_of_tile, tables)
            new_k.append(kv[:n_p, :KV_WIDTH].reshape(batch, seq, N_KV_HEADS, HEAD_DIM))
            new_v.append(kv[:n_p, KV_WIDTH:].reshape(batch, seq, N_KV_HEADS, HEAD_DIM))
            sink_b = jnp.broadcast_to(sink[j][:, None], (N_Q_HEADS, LANES))
            att_p = _ctx_attention(proj, sink_b, batch, seq)
            ctx_k = cache_k[:, j].reshape(dec_batch, past, KV_WIDTH).astype(BF16)
            ctx_v = cache_v[:, j].reshape(dec_batch, past, KV_WIDTH).astype(BF16)
            att_s = _lat_attention(proj, sink_b, ctx_k, ctx_v, n_p, dec_batch, dec_seq)
            mix_a = jnp.concatenate([att_p, att_s], axis=0)
            mix_b = _sgu(proj, g_sgu[j], w_spatial[j].astype(BF16), b_spatial[j].T)
        else:
            proj = _inproj(x, g_norm1[l], sc1, sh1, w_in_cd[j].astype(BF16), cond_of_tile)
            wa = _block_diag_pairs(w_rg_a[j])
            wi = _block_diag_pairs(w_rg_i[j])
            w4 = jnp.concatenate([wa[0], wi[0], wa[1], wi[1]], axis=-1).astype(BF16)
            strips = lambda v: v.reshape(RNN_WIDTH // LANES, 1, LANES)
            b4 = jnp.concatenate([strips(b_rg_a[j, 0]), strips(b_rg_i[j, 0]),
                                  strips(b_rg_a[j, 1]), strips(b_rg_i[j, 1])], axis=-1)
            h0_p = jnp.zeros((batch, 2, RNN_WIDTH), F32)
            c_p, h_fin = _rglru(proj, 0, batch, seq, conv_c_w[j], conv_c_b[j], w4, b4, lam[j], h0_p)
            c_s, _ = _rglru(proj, n_p, dec_batch, dec_seq, conv_c_w[j], conv_c_b[j], w4, b4, lam[j], state_h[:, j])
            new_h.append(h_fin)
            w_pad = jnp.concatenate([conv_d_w[j], jnp.zeros((4 * SUBLANES - CONV_K, CONV_WIDTH), F32)], axis=0)
            d_p = _convmod(proj, 0, batch, seq, w_pad, conv_d_b[j], ln_d_g[j], ln_d_b[j])
            d_s = _convmod(proj, n_p, dec_batch, dec_seq, w_pad, conv_d_b[j], ln_d_g[j], ln_d_b[j])
            mix_a = jnp.concatenate([c_p, c_s], axis=0)
            mix_b = jnp.concatenate([d_p, d_s], axis=0)
        w_r = jnp.concatenate([w_router[l], jnp.zeros((d, LANES - N_EXPERTS), F32)], axis=1)
        w_r_hi = w_r.astype(BF16)
        w_r = jnp.concatenate([w_r_hi, (w_r - w_r_hi.astype(F32)).astype(BF16)], axis=1)
        b_r = jnp.concatenate([b_router[l], jnp.full((LANES - N_EXPERTS,), NEG_INF, F32)]).reshape(1, LANES)
        x, h_tiles, logits = _outproj(x, mix_a, mix_b, w_out[l].astype(BF16), g1, g_norm2[l], sc2, sh2, w_r, b_r, cond_of_tile)
        x = _moe(x, h_tiles, logits, g2, g_final, l, w_up, b_up, w_down, b_down, cond_of_tile, final=(l == depth - 1))

    y_prompt = x[:n_p].reshape(batch, seq, d)
    y_sample = x[n_p:].reshape(dec_batch, dec_seq, d)
    return (y_prompt, y_sample, jnp.stack(new_k, axis=1), jnp.stack(new_v, axis=1), jnp.stack(new_h, axis=1))
```

```python
import functools

import jax
import jax.numpy as jnp
from jax import lax
from jax.experimental import pallas as pl
from jax.experimental.pallas import tpu as pltpu

F32 = jnp.float32
BF16 = jnp.bfloat16

HEAD_DIM = 64
N_Q_HEADS = 8
N_KV_HEADS = 2
Q_PER_KV = N_Q_HEADS // N_KV_HEADS
ATT_WIDTH = N_Q_HEADS * HEAD_DIM
KV_WIDTH = N_KV_HEADS * HEAD_DIM
BLOCK = 128
GRID_W = 64
ROPE_BASE = 10000.0
N_SGU_GROUPS = 8
SGU_GROUP = 64
SGU_WIDTH = N_SGU_GROUPS * SGU_GROUP
CHUNK = 128
RNN_BLOCK = 64
RNN_WIDTH = 512
RNN_CONV = 4
RG_C = 8.0
CONV_WIDTH = 512
CONV_K = 31
N_EXPERTS = 32
TOP_K = 4
SWIGLU_LIMIT = 7.0
SWIGLU_ALPHA = 1.702
N_MOD = 6
EPS = 1e-6
NEG_INF = -1e30

LANES = 128
SUBLANES = 8
VMEM_LIMIT = 52 * 1024 * 1024

ROW_TILE = 512
EXPERT_TILE = 256
COMBINE_TILE = 256
COPY_UNROLL = 8
FF_CHUNKS = 4
ID_BITS = 17
COPY_SHARE = (0, 96, 192, 256, 256)
CONV_HALO = 16
SCAN_UNROLL = 4


def _params(*sem):
    return pltpu.CompilerParams(dimension_semantics=sem, vmem_limit_bytes=VMEM_LIMIT)


def _gelu(x):
    return jax.nn.gelu(x, approximate=True)


def _rms_mod(x, g, sc, sh):
    h = x * lax.rsqrt(jnp.mean(x * x, axis=-1, keepdims=True) + EPS) * g
    return h * (1.0 + sc) + sh


def _adaln_kernel(c_ref, w_ref, b_ref, o_ref):
    c = c_ref[...]
    s = c * jax.nn.sigmoid(c)
    o_ref[...] = jnp.dot(s.astype(BF16), w_ref[...].astype(BF16), preferred_element_type=F32) + b_ref[...]


def _adaln(cond8, w_ada, b_ada):
    depth, d, n = w_ada.shape
    tn = n // 4
    return pl.pallas_call(
        _adaln_kernel,
        grid=(depth, n // tn),
        in_specs=[pl.BlockSpec((8, d), lambda l, j: (0, 0)),
                  pl.BlockSpec((None, d, tn), lambda l, j: (l, 0, j)),
                  pl.BlockSpec((None, 1, tn), lambda l, j: (l, 0, j))],
        out_specs=pl.BlockSpec((None, 8, tn), lambda l, j: (l, 0, j)),
        out_shape=jax.ShapeDtypeStruct((depth, 8, n), F32),
        compiler_params=_params("arbitrary", "arbitrary"),
        name="adaln",
    )(cond8, w_ada, b_ada.reshape(depth, 1, n))


def _inproj_kernel(x_ref, g_ref, sc_ref, sh_ref, w_ref, *rest, rope):
    h = _rms_mod(x_ref[...], g_ref[...], sc_ref[...], sh_ref[...])
    acc = jnp.dot(h.astype(BF16), w_ref[...], preferred_element_type=F32)
    if not rope:
        (o_ref,) = rest
        o_ref[...] = acc.astype(BF16)
        return
    cos_ref, sin_ref, o_ref, kv_ref = rest
    kv0 = ATT_WIDTH + 2 * SGU_WIDTH
    kv_ref[...] = acc[:, kv0:kv0 + 2 * KV_WIDTH]
    cos = cos_ref[...]
    sin = sin_ref[...]
    lane = lax.broadcasted_iota(jnp.int32, cos.shape, 1)
    first = (lane % 32) < 16
    o_ref[...] = acc.astype(BF16)
    for j in (0, 1, 2, 3, kv0 // LANES):
        seg = acc[:, LANES * j:LANES * (j + 1)]
        partner = jnp.where(first, pltpu.roll(seg, LANES - 16, 1), pltpu.roll(seg, 16, 1))
        o_ref[:, LANES * j:LANES * (j + 1)] = (seg * cos + partner * sin).astype(BF16)


def _inproj(x, g, sc, sh, w, cond_of_tile, tables=None):
    n, d = x.shape
    nout = w.shape[1]
    tm = ROW_TILE
    rope = tables is not None
    in_specs = [pl.BlockSpec((tm, d), lambda i: (i, 0)),
                pl.BlockSpec((1, d), lambda i: (0, 0)),
                pl.BlockSpec((None, 1, d), lambda i: (cond_of_tile(i), 0, 0)),
                pl.BlockSpec((None, 1, d), lambda i: (cond_of_tile(i), 0, 0)),
                pl.BlockSpec((d, nout), lambda i: (0, 0))]
    args = [x, g.reshape(1, d), sc, sh, w]
    out_specs = pl.BlockSpec((tm, nout), lambda i: (i, 0))
    out_shape = jax.ShapeDtypeStruct((n, nout), BF16)
    if rope:
        in_specs += [pl.BlockSpec((tm, LANES), lambda i: (i, 0))] * 2
        args += list(tables)
        out_specs = [out_specs, pl.BlockSpec((tm, 2 * KV_WIDTH), lambda i: (i, 0))]
        out_shape = [out_shape, jax.ShapeDtypeStruct((n, 2 * KV_WIDTH), F32)]
    return pl.pallas_call(
        functools.partial(_inproj_kernel, rope=rope),
        grid=(n // tm,),
        in_specs=in_specs, out_specs=out_specs, out_shape=out_shape,
        compiler_params=_params("arbitrary"),
        name="inproj_rope" if rope else "inproj",
    )(*args)


def _attend(qh, keys, values, masks, sink):
    scale = HEAD_DIM ** -0.5
    scores = []
    for kk, mk in zip(keys, masks):
        s = lax.dot_general(qh, kk, (((1,), (1,)), ((), ())), preferred_element_type=F32) * scale
        if mk is not None:
            s = jnp.where(mk, s, NEG_INF)
        scores.append(s)
    m = sink
    for s in scores:
        m = jnp.maximum(m, jnp.max(s, axis=-1, keepdims=True))
    es = [jnp.exp(s - m) for s in scores]
    den = jnp.exp(sink - m)
    for e in es:
        den = den + jnp.sum(e, axis=-1, keepdims=True)
    inv = 1.0 / den
    out = None
    for e, vv in zip(es, values):
        o = jnp.dot((e * inv).astype(BF16), vv, preferred_element_type=F32)
        out = o if out is None else out + o
    return out


def _attend_group(q, g, kg, vg, valid, sink_ref, o_ref):
    rows = q.shape[0]
    heads = range(Q_PER_KV * g, Q_PER_KV * (g + 1))
    qs = jnp.concatenate([q[:, HEAD_DIM * h:HEAD_DIM * (h + 1)] for h in heads], axis=0)
    sink = jnp.concatenate([jnp.broadcast_to(sink_ref[h:h + 1, 0:1], (rows, 1)) for h in heads], axis=0)
    out = _attend(qs, [kg], [vg], [valid], sink)
    for i, h in enumerate(heads):
        o_ref[:, HEAD_DIM * h:HEAD_DIM * (h + 1)] = out[i * rows:(i + 1) * rows].astype(BF16)


def _ctx_attn_kernel(sink_ref, q_ref, kv_ref, o_ref):
    q = q_ref[...]
    kv = kv_ref[...]
    for h in range(N_Q_HEADS):
        g = h // Q_PER_KV
        qh = q[:, HEAD_DIM * h:HEAD_DIM * (h + 1)]
        kg = kv[:, HEAD_DIM * g:HEAD_DIM * (g + 1)]
        vg = kv[:, KV_WIDTH + HEAD_DIM * g:KV_WIDTH + HEAD_DIM * (g + 1)]
        out = _attend(qh, [kg], [vg], [None], sink_ref[h:h + 1, 0:1])
        o_ref[:, HEAD_DIM * h:HEAD_DIM * (h + 1)] = out.astype(BF16)


def _ctx_attention(proj, sink_b, batch, seq):
    kv_blk = (ATT_WIDTH + 2 * SGU_WIDTH) // (2 * KV_WIDTH)
    return pl.pallas_call(
        _ctx_attn_kernel,
        grid=(batch,),
        in_specs=[pl.BlockSpec((N_Q_HEADS, LANES), lambda b: (0, 0)),
                  pl.BlockSpec((seq, ATT_WIDTH), lambda b: (b, 0)),
                  pl.BlockSpec((seq, 2 * KV_WIDTH), lambda b: (b, kv_blk))],
        out_specs=pl.BlockSpec((seq, ATT_WIDTH), lambda b: (b, 0)),
        out_shape=jax.ShapeDtypeStruct((batch * seq, ATT_WIDTH), BF16),
        compiler_params=_params("arbitrary"),
        name="ctx_attention",
    )(sink_b, proj, proj)


def _lat_attn_kernel(sink_ref, q_ref, kvp_ref, kvc_ref, kvn_ref, ck_ref, cv_ref, o_ref):
    n = pl.program_id(1)
    nb = pl.num_programs(1)
    past = ck_ref.shape[0]
    q = q_ref[...]
    k_all = jnp.concatenate([kvp_ref[:, :KV_WIDTH], kvc_ref[:, :KV_WIDTH], kvn_ref[:, :KV_WIDTH], ck_ref[...]], axis=0)
    v_all = jnp.concatenate([kvp_ref[:, KV_WIDTH:], kvc_ref[:, KV_WIDTH:], kvn_ref[:, KV_WIDTH:], cv_ref[...]], axis=0)
    shape = (Q_PER_KV * BLOCK, 3 * BLOCK + past)
    qi = lax.broadcasted_iota(jnp.int32, shape, 0) % BLOCK
    kj = lax.broadcasted_iota(jnp.int32, shape, 1)
    lo = jnp.where(n > 0, 0, BLOCK)
    hi = jnp.where(n < nb - 1, 3 * BLOCK, 2 * BLOCK)
    valid = ((kj >= qi) & (kj <= qi + 2 * BLOCK) & (kj >= lo) & (kj < hi)) | (kj >= 3 * BLOCK)
    for g in range(N_KV_HEADS):
        kg = k_all[:, HEAD_DIM * g:HEAD_DIM * (g + 1)]
        vg = v_all[:, HEAD_DIM * g:HEAD_DIM * (g + 1)]
        _attend_group(q, g, kg, vg, valid, sink_ref, o_ref)


def _lat_attention(proj, sink_b, ctx_k, ctx_v, row0, batch, length):
    nb = length // BLOCK
    b0 = row0 // BLOCK
    kv_blk = (ATT_WIDTH + 2 * SGU_WIDTH) // (2 * KV_WIDTH)
    past = ctx_k.shape[1]
    kv_spec = lambda f: pl.BlockSpec((BLOCK, 2 * KV_WIDTH), lambda b, n: (b0 + b * nb + f(n), kv_blk))
    return pl.pallas_call(
        _lat_attn_kernel,
        grid=(batch, nb),
        in_specs=[pl.BlockSpec((N_Q_HEADS, LANES), lambda b, n: (0, 0)),
                  pl.BlockSpec((BLOCK, ATT_WIDTH), lambda b, n: (b0 + b * nb + n, 0)),
                  kv_spec(lambda n: jnp.maximum(n - 1, 0)),
                  kv_spec(lambda n: n),
                  kv_spec(lambda n: jnp.minimum(n + 1, nb - 1)),
                  pl.BlockSpec((None, past, KV_WIDTH), lambda b, n: (b, 0, 0)),
                  pl.BlockSpec((None, past, KV_WIDTH), lambda b, n: (b, 0, 0))],
        out_specs=pl.BlockSpec((BLOCK, ATT_WIDTH), lambda b, n: (b * nb + n, 0)),
        out_shape=jax.ShapeDtypeStruct((batch * length, ATT_WIDTH), BF16),
        compiler_params=_params("arbitrary", "arbitrary"),
        name="latent_attention",
    )(sink_b, proj, proj, proj, proj, ctx_k, ctx_v)


def _sgu_kernel(u_ref, v_ref, g_ref, ws_ref, bs_ref, o_ref):
    rows = u_ref.shape[0]
    v = _gelu(v_ref[...].astype(F32))
    mu = jnp.mean(v, axis=-1, keepdims=True)
    vc = v - mu
    v = vc * lax.rsqrt(jnp.mean(vc * vc, axis=-1, keepdims=True) + EPS) * g_ref[...]
    vb = v.astype(BF16)
    for c in range(rows // CHUNK):
        r0 = c * CHUNK
        for g in range(N_SGU_GROUPS):
            c0 = g * SGU_GROUP
            mixed = jnp.dot(ws_ref[g], vb[r0:r0 + CHUNK, c0:c0 + SGU_GROUP], preferred_element_type=F32)
            mixed = mixed + bs_ref[:, g:g + 1]
            u = _gelu(u_ref[r0:r0 + CHUNK, c0:c0 + SGU_GROUP].astype(F32))
            o_ref[r0:r0 + CHUNK, c0:c0 + SGU_GROUP] = (u * mixed).astype(BF16)


def _sgu(proj, g_sgu, w_s, b_s_t):
    n = proj.shape[0]
    tm = 2 * CHUNK
    return pl.pallas_call(
        _sgu_kernel,
        grid=(n // tm,),
        in_specs=[pl.BlockSpec((tm, SGU_WIDTH), lambda i: (i, 1)),
                  pl.BlockSpec((tm, SGU_WIDTH), lambda i: (i, 2)),
                  pl.BlockSpec((1, SGU_WIDTH), lambda i: (0, 0)),
                  pl.BlockSpec((N_SGU_GROUPS, CHUNK, CHUNK), lambda i: (0, 0, 0)),
                  pl.BlockSpec((CHUNK, N_SGU_GROUPS), lambda i: (0, 0))],
        out_specs=pl.BlockSpec((tm, SGU_WIDTH), lambda i: (i, 0)),
        out_shape=jax.ShapeDtypeStruct((n, SGU_WIDTH), BF16),
        compiler_params=_params("arbitrary"),
        name="sgu",
    )(proj, proj, g_sgu.reshape(1, SGU_WIDTH), w_s, b_s_t)


def _scan8(a, b, reverse):
    row = lax.broadcasted_iota(jnp.int32, a.shape, 0)
    for d in (1, 2, 4):
        if reverse:
            keep = row < SUBLANES - d
            shift = SUBLANES - d
        else:
            keep = row >= d
            shift = d
        a_sh = jnp.where(keep, pltpu.roll(a, shift, 0), 1.0)
        b_sh = jnp.where(keep, pltpu.roll(b, shift, 0), 0.0)
        b = b + a * b_sh
        a = a * a_sh
    return a, b


def _rglru_kernel(gate_ref, xr_ref, cw_ref, cb_ref, w4_ref, b4_ref, lam_ref, h0_ref, o_ref, hl_ref,
                  xpad, a_f, b_f, a_b, b_b, y_f, y_b):
    length = xr_ref.shape[0]
    xpad[0:SUBLANES, :] = jnp.zeros((SUBLANES, LANES), F32)
    xpad[SUBLANES:SUBLANES + length, :] = xr_ref[...].astype(F32)
    xpad[SUBLANES + length:, :] = jnp.zeros((SUBLANES, LANES), F32)
    left = RNN_CONV // 2
    xc = cb_ref[...] + cw_ref[0:1, :] * xpad[SUBLANES - left:SUBLANES - left + length, :]
    for j in range(1, RNN_CONV):
        xc = xc + cw_ref[j:j + 1, :] * xpad[SUBLANES - left + j:SUBLANES - left + j + length, :]
    pre = jnp.dot(xc.astype(BF16), w4_ref[...], preferred_element_type=F32) + b4_ref[...]
    lam = lam_ref[...]
    sp = jnp.maximum(-lam, 0.0) + jnp.log(1.0 + jnp.exp(-jnp.abs(lam)))
    for d, (a_ref, b_ref) in enumerate(((a_f, b_f), (a_b, b_b))):
        r = jax.nn.sigmoid(pre[:, 2 * d * LANES:(2 * d + 1) * LANES])
        i = jax.nn.sigmoid(pre[:, (2 * d + 1) * LANES:(2 * d + 2) * LANES])
        log_a = -RG_C * r * sp[d:d + 1, :]
        a = jnp.exp(log_a)
        a_ref[...] = a
        b_ref[...] = jnp.sqrt(1.0 - a * a) * (i * xc)

    groups = length // SUBLANES

    def body(it, carry):
        hf, hb = carry
        for v in range(SCAN_UNROLL):
            g = it * SCAN_UNROLL + v
            rf = pl.multiple_of(g * SUBLANES, SUBLANES)
            rb = pl.multiple_of((groups - 1 - g) * SUBLANES, SUBLANES)
            af, bf = _scan8(a_f[pl.ds(rf, SUBLANES), :], b_f[pl.ds(rf, SUBLANES), :], False)
            ab, bb = _scan8(a_b[pl.ds(rb, SUBLANES), :], b_b[pl.ds(rb, SUBLANES), :], True)
            yf = bf + af * hf
            yb = bb + ab * hb
            y_f[pl.ds(rf, SUBLANES), :] = yf
            y_b[pl.ds(rb, SUBLANES), :] = yb
            hf, hb = yf[SUBLANES - 1:SUBLANES, :], yb[0:1, :]
        return hf, hb

    hf, hb = lax.fori_loop(0, groups // SCAN_UNROLL, body, (h0_ref[0:1, :], h0_ref[1:2, :]))
    hl_ref[0:1, :] = hf
    hl_ref[1:2, :] = hb
    o_ref[...] = (_gelu(gate_ref[...].astype(F32)) * (y_f[...] + y_b[...])).astype(BF16)


def _rglru(proj, row0, batch, length, conv_w, conv_b, w4, b4, lam, h0):
    strips = RNN_WIDTH // LANES
    r0 = row0 // length
    seq = lambda c0: pl.BlockSpec((length, LANES), lambda b, j: (r0 + b, c0 + j))
    strip = lambda rows: pl.BlockSpec((rows, LANES), lambda b, j: (0, j))
    scratch = [pltpu.VMEM((length + 2 * SUBLANES, LANES), F32)] + [pltpu.VMEM((length, LANES), F32)] * 6
    return pl.pallas_call(
        _rglru_kernel,
        grid=(batch, strips),
        in_specs=[seq(0), seq(strips), strip(RNN_CONV), strip(1),
                  pl.BlockSpec((None, LANES, 4 * LANES), lambda b, j: (j, 0, 0)),
                  pl.BlockSpec((None, 1, 4 * LANES), lambda b, j: (j, 0, 0)),
                  strip(2),
                  pl.BlockSpec((None, 2, LANES), lambda b, j: (b, 0, j))],
        out_specs=[pl.BlockSpec((length, LANES), lambda b, j: (b, j)),
                   pl.BlockSpec((None, 2, LANES), lambda b, j: (b, 0, j))],
        out_shape=[jax.ShapeDtypeStruct((batch * length, RNN_WIDTH), BF16),
                   jax.ShapeDtypeStruct((batch, 2, RNN_WIDTH), F32)],
        scratch_shapes=scratch,
        compiler_params=_params("arbitrary", "arbitrary"),
        name="rglru",
    )(proj, proj, conv_w, conv_b.reshape(1, RNN_WIDTH), w4, b4, lam, h0)


def _convmod_kernel(a_ref, b_ref, pa_ref, pb_ref, na_ref, nb_ref, w_ref, cb_ref, g_ref, beta_ref, o_ref, zpad):
    t = pl.program_id(1)
    nt = pl.num_programs(1)
    rows = a_ref.shape[0]

    def glu(x_ref, y_ref):
        return x_ref[...].astype(F32) * jax.nn.sigmoid(y_ref[...].astype(F32))

    zpad[0:CONV_HALO, :] = glu(pa_ref, pb_ref) * (t > 0).astype(F32)
    zpad[CONV_HALO:CONV_HALO + rows, :] = glu(a_ref, b_ref)
    zpad[CONV_HALO + rows:, :] = glu(na_ref, nb_ref) * (t < nt - 1).astype(F32)
    off = CONV_HALO - CONV_K // 2
    acc = cb_ref[...] + w_ref[0:1, :] * zpad[off:off + rows, :]
    for j in range(1, CONV_K):
        acc = acc + w_ref[j:j + 1, :] * zpad[off + j:off + j + rows, :]
    mu = jnp.mean(acc, axis=-1, keepdims=True)
    xc = acc - mu
    y = xc * lax.rsqrt(jnp.mean(xc * xc, axis=-1, keepdims=True) + EPS) * g_ref[...] + beta_ref[...]
    o_ref[...] = (y * jax.nn.sigmoid(y)).astype(BF16)


def _convmod(proj, row0, batch, length, w_pad, cb, ln_g, ln_b):
    tl = min(length, ROW_TILE)
    nt = length // tl
    r0 = row0 // tl
    hb = tl // CONV_HALO
    h0 = row0 // CONV_HALO
    nh = (batch * length + row0) // CONV_HALO
    ca, cbk = 2, 3
    main = lambda c: pl.BlockSpec((tl, CONV_WIDTH), lambda b, t: (r0 + b * nt + t, c))
    prev = lambda c: pl.BlockSpec((CONV_HALO, CONV_WIDTH),
                                  lambda b, t: (jnp.maximum(h0 + (b * nt + t) * hb - 1, 0), c))
    nxt = lambda c: pl.BlockSpec((CONV_HALO, CONV_WIDTH),
                                 lambda b, t: (jnp.minimum(h0 + (b * nt + t + 1) * hb, nh - 1), c))
    vec = pl.BlockSpec((1, CONV_WIDTH), lambda b, t: (0, 0))
    return pl.pallas_call(
        _convmod_kernel,
        grid=(batch, nt),
        in_specs=[main(ca), main(cbk), prev(ca), prev(cbk), nxt(ca), nxt(cbk),
                  pl.BlockSpec((4 * SUBLANES, CONV_WIDTH), lambda b, t: (0, 0)), vec, vec, vec],
        out_specs=pl.BlockSpec((tl, CONV_WIDTH), lambda b, t: (b * nt + t, 0)),
        out_shape=jax.ShapeDtypeStruct((batch * length, CONV_WIDTH), BF16),
        scratch_shapes=[pltpu.VMEM((tl + 2 * CONV_HALO, CONV_WIDTH), F32)],
        compiler_params=_params("arbitrary", "arbitrary"),
        name="convmod",
    )(proj, proj, proj, proj, proj, proj, w_pad, cb.reshape(1, -1), ln_g.reshape(1, -1), ln_b.reshape(1, -1))


def _store_row_tiles(ref, val):
    rows, d = val.shape
    seg = d // LANES
    for s in range(seg):
        ref[pl.ds(s, rows, stride=seg), :] = val[:, LANES * s:LANES * (s + 1)]


def _load_row_tiles(ref, rows, seg, base=0):
    return jnp.concatenate([ref[pl.ds(base + s, rows, stride=seg), :] for s in range(seg)], axis=-1)


def _outproj_kernel(x_ref, ma_ref, mb_ref, wo_ref, g1_ref, gn_ref, sc_ref, sh_ref, wr_ref, br_ref,
                    xo_ref, h_ref, lg_ref):
    half = ma_ref.shape[1]
    y = jnp.dot(ma_ref[...], wo_ref[0:half, :], preferred_element_type=F32)
    y = y + jnp.dot(mb_ref[...], wo_ref[half:, :], preferred_element_type=F32)
    x = x_ref[...] + g1_ref[...] * y
    xo_ref[...] = x
    h = _rms_mod(x, gn_ref[...], sc_ref[...], sh_ref[...])
    _store_row_tiles(h_ref, h)
    h_hi = h.astype(BF16)
    h_lo = (h - h_hi.astype(F32)).astype(BF16)
    r = jnp.dot(h_hi, wr_ref[...], preferred_element_type=F32)
    r = r[:, :LANES] + r[:, LANES:] + jnp.dot(h_lo, wr_ref[:, :LANES], preferred_element_type=F32)
    lg_ref[...] = r + br_ref[...]


def _outproj(x, mix_a, mix_b, w_out, g1, gn, sc, sh, w_r, b_r, cond_of_tile):
    n, d = x.shape
    tm = ROW_TILE
    seg = d // LANES
    half = mix_a.shape[1]
    row = lambda w: pl.BlockSpec((tm, w), lambda i: (i, 0))
    const = lambda r, c: pl.BlockSpec((r, c), lambda i: (0, 0))
    mod = pl.BlockSpec((None, 1, d), lambda i: (cond_of_tile(i), 0, 0))
    return pl.pallas_call(
        _outproj_kernel,
        grid=(n // tm,),
        in_specs=[row(d), row(half), row(half), const(2 * half, d), mod, const(1, d), mod, mod,
                  const(d, 2 * LANES), const(1, LANES)],
        out_specs=[row(d), pl.BlockSpec((tm * seg, LANES), lambda i: (i, 0)), row(LANES)],
        out_shape=[jax.ShapeDtypeStruct((n, d), F32), jax.ShapeDtypeStruct((n * seg, LANES), F32),
                   jax.ShapeDtypeStruct((n, LANES), F32)],
        compiler_params=_params("arbitrary"),
        name="outproj",
    )(x, mix_a, mix_b, w_out, g1, gn.reshape(1, d), sc, sh, w_r, b_r)


def _route_kernel(lg_ref, idx_ref, gate_ref, cnt_ref, carry):
    @pl.when(pl.program_id(0) == 0)
    def _():
        carry[...] = jnp.zeros_like(carry)

    lg = lg_ref[...]
    lane = lax.broadcasted_iota(jnp.int32, lg.shape, 1).astype(F32)
    work = lg
    vals, idxs = [], []
    for _ in range(TOP_K):
        m = jnp.max(work, axis=-1, keepdims=True)
        ik = jnp.min(jnp.where(work == m, lane, float(LANES)), axis=-1, keepdims=True)
        vals.append(m)
        idxs.append(ik)
        work = jnp.where(lane == ik, -jnp.inf, work)
    es = [jnp.exp(v - vals[0]) for v in vals]
    den = es[0] + es[1] + es[2] + es[3]
    onehot = jnp.zeros(lg.shape, F32)
    for k in range(TOP_K):
        idx_ref[:, k:k + 1] = idxs[k].astype(jnp.int32)
        gate_ref[:, k:k + 1] = es[k] / den
        onehot = onehot + (lane == idxs[k]).astype(F32)
    carry[...] = carry[...] + jnp.sum(onehot, axis=0, keepdims=True)
    cnt_ref[...] = carry[...]


def _route(logits):
    n = logits.shape[0]
    tm = ROW_TILE
    small = lambda dt: jax.ShapeDtypeStruct((n, TOP_K), dt)
    return pl.pallas_call(
        _route_kernel,
        grid=(n // tm,),
        in_specs=[pl.BlockSpec((tm, LANES), lambda i: (i, 0))],
        out_specs=[pl.BlockSpec((tm, TOP_K), lambda i: (i, 0))] * 2 + [pl.BlockSpec((1, LANES), lambda i: (0, 0))],
        out_shape=[small(jnp.int32), small(F32), jax.ShapeDtypeStruct((1, LANES), F32)],
        scratch_shapes=[pltpu.VMEM((1, LANES), F32)],
        compiler_params=_params("arbitrary"),
        name="route",
    )(logits)


def _row_tile(ref, r):
    return ref.at[pl.ds(pl.multiple_of(r * SUBLANES, SUBLANES), SUBLANES)]


def _expert_kernel(a_ref, te_ref, nt_ref, h_ref, wu_ref, bu_ref, wd_ref, bd_ref, ys_ref,
                   xbuf0, xbuf1, ybuf0, ybuf1, xb, yacc, wu_bf, wd_bf, sem_g, sem_s):
    j = pl.program_id(0)
    tm = EXPERT_TILE
    n_tok = h_ref.shape[0] // SUBLANES
    n_tiles = nt_ref[0]
    xbufs, ybufs = (xbuf0, xbuf1), (ybuf0, ybuf1)
    assert len(COPY_SHARE) == FF_CHUNKS + 1 and COPY_SHARE[-1] == tm
    cw = wd_bf.shape[1]

    def gather(tile, s, i):
        tok = lax.shift_right_logical(a_ref[(tile + 1) * tm + i], ID_BITS)
        return pltpu.make_async_copy(_row_tile(h_ref, tok), _row_tile(xbufs[s], i), sem_g.at[s])

    def scatter(tile, s, i):
        row = a_ref[(tile + 1) * tm + i] & ((1 << ID_BITS) - 1)
        return pltpu.make_async_copy(_row_tile(ybufs[s], i), _row_tile(ys_ref, row), sem_s.at[s])

    def in_loop(make, act):
        def body(g, carry):
            for u in range(COPY_UNROLL):
                act(make(g * COPY_UNROLL + u))
            return carry
        lax.fori_loop(0, tm // COPY_UNROLL, body, 0)

    @pl.when(j == 0)
    def _():
        ybuf1[...] = jnp.zeros_like(ybuf1)
        in_loop(lambda i: gather(0, 0, i), lambda c: c.start())

    changed = jnp.logical_or(j == 0, te_ref[j] != te_ref[jnp.maximum(j - 1, 0)])

    @pl.when(jnp.logical_and(j <= n_tiles, changed))
    def _():
        for c in range(FF_CHUNKS):
            wu_bf[c] = wu_ref[:, c * cw:(c + 1) * cw].astype(BF16)
            wu_bf[FF_CHUNKS + c] = wu_ref[:, (FF_CHUNKS + c) * cw:(FF_CHUNKS + c + 1) * cw].astype(BF16)
            wd_bf[c] = wd_ref[c * cw:(c + 1) * cw, :].astype(BF16)

    for s in range(2):
        @pl.when(jnp.logical_and(j <= n_tiles + 1, j % 2 == s))
        def _():
            in_loop(lambda i: gather(j, s, i), lambda c: c.wait())

            @pl.when(j >= 1)
            def _():
                in_loop(lambda i: scatter(j - 2, s, i), lambda c: c.wait())

        @pl.when(jnp.logical_and(j <= n_tiles, j % 2 == s))
        def _():
            xb[...] = _load_row_tiles(xbufs[s], tm, SUBLANES).astype(BF16)

        for c in range(FF_CHUNKS):
            @pl.when(jnp.logical_and(jnp.logical_and(j <= n_tiles, j % 2 == s), te_ref[c] >= 0))
            def _():
                for i in range(COPY_SHARE[c], COPY_SHARE[c + 1]):
                    gather(j + 1, 1 - s, i).start()
                    scatter(j - 1, 1 - s, i).start()
                x = xb[...]
                g = jnp.dot(x, wu_bf[c], preferred_element_type=F32) + bu_ref[c]
                u = jnp.dot(x, wu_bf[FF_CHUNKS + c], preferred_element_type=F32) + bu_ref[FF_CHUNKS + c]
                g = jnp.minimum(g, SWIGLU_LIMIT)
                u = jnp.clip(u, -SWIGLU_LIMIT, SWIGLU_LIMIT)
                act = (u + 1.0) * (g * jax.nn.sigmoid(SWIGLU_ALPHA * g))
                part = jnp.dot(act.astype(BF16), wd_bf[c], preferred_element_type=F32)
                if c == 0:
                    yacc[...] = part + bd_ref[...]
                elif c < FF_CHUNKS - 1:
                    yacc[...] += part
                else:
                    _store_row_tiles(ybufs[s], yacc[...] + part)


def _experts(a_ext, tile_expert, n_tiles, h_tiles, layer, w_up, b_up, w_down, b_down):
    depth, n_exp, d, ff2 = w_up.shape
    ff = w_down.shape[2]
    tm = EXPERT_TILE
    n_tok = h_tiles.shape[0] // SUBLANES
    pick = lambda j, a, te, nt: (layer, te[j], 0, 0)
    cw = ff // FF_CHUNKS
    tile_buf = pltpu.VMEM((tm * SUBLANES, LANES), F32)
    grid_spec = pltpu.PrefetchScalarGridSpec(
        num_scalar_prefetch=3, grid=(tile_expert.shape[0],),
        in_specs=[pl.BlockSpec(memory_space=pl.ANY),
                  pl.BlockSpec((None, None, d, ff2), pick),
                  pl.BlockSpec((None, None, 2 * FF_CHUNKS, 1, cw), lambda j, a, te, nt: (layer, te[j], 0, 0, 0)),
                  pl.BlockSpec((None, None, ff, d), pick),
                  pl.BlockSpec((None, None, 1, d), pick)],
        out_specs=pl.BlockSpec(memory_space=pl.ANY),
        scratch_shapes=[tile_buf, tile_buf, tile_buf, tile_buf,
                        pltpu.VMEM((tm, d), BF16), pltpu.VMEM((tm, d), F32),
                        pltpu.VMEM((2 * FF_CHUNKS, d, cw), BF16), pltpu.VMEM((FF_CHUNKS, cw, d), BF16),
                        pltpu.SemaphoreType.DMA((2,)), pltpu.SemaphoreType.DMA((2,))])
    return pl.pallas_call(
        _expert_kernel,
        grid_spec=grid_spec,
        out_shape=jax.ShapeDtypeStruct(((TOP_K * n_tok + tm) * SUBLANES, LANES), F32),
        compiler_params=pltpu.CompilerParams(dimension_semantics=("arbitrary",), has_side_effects=True,
                                             vmem_limit_bytes=VMEM_LIMIT),
        name="experts",
    )(a_ext, tile_expert, n_tiles, h_tiles,
      w_up, b_up.reshape(depth, n_exp, 2 * FF_CHUNKS, 1, cw), w_down, b_down.reshape(depth, n_exp, 1, d))


def _combine_kernel(x_ref, gate_ref, g2_ref, gf_ref, y_ref, o_ref, *, final):
    tc = x_ref.shape[0]
    gates = gate_ref[...]
    acc = gates[:, 0:1] * _load_row_tiles(y_ref, tc, SUBLANES)
    for k in range(1, TOP_K):
        acc = acc + gates[:, k:k + 1] * _load_row_tiles(y_ref, tc, SUBLANES, base=k * tc * SUBLANES)
    x = x_ref[...] + g2_ref[...] * acc
    if final:
        x = x * lax.rsqrt(jnp.mean(x * x, axis=-1, keepdims=True) + EPS) * gf_ref[...]
    o_ref[...] = x


def _combine(x, gates, g2, g_final, ys, cond_of_tile, final):
    n, d = x.shape
    tc = COMBINE_TILE
    per_row_tile = ROW_TILE // tc
    return pl.pallas_call(
        functools.partial(_combine_kernel, final=final),
        grid=(n // tc,),
        in_specs=[pl.BlockSpec((tc, d), lambda i: (i, 0)),
                  pl.BlockSpec((tc, TOP_K), lambda i: (i, 0)),
                  pl.BlockSpec((None, 1, d), lambda i: (cond_of_tile(i // per_row_tile), 0, 0)),
                  pl.BlockSpec((1, d), lambda i: (0, 0)),
                  pl.BlockSpec((tc * TOP_K * SUBLANES, LANES), lambda i: (i, 0))],
        out_specs=pl.BlockSpec((tc, d), lambda i: (i, 0)),
        out_shape=jax.ShapeDtypeStruct((n, d), F32),
        compiler_params=_params("arbitrary"),
        name="combine_final" if final else "combine",
    )(x, gates, g2, g_final.reshape(1, d), ys)


def _moe(x, h_tiles, logits, g2, g_final, layer, w_up, b_up, w_down, b_down, cond_of_tile, final):
    n = x.shape[0]
    tm = EXPERT_TILE
    idx, gates, counts = _route(logits)
    n_ids = n * TOP_K
    id_bits = ID_BITS
    assert n_ids + tm <= (1 << ID_BITS) and n <= (1 << (32 - ID_BITS))
    tok = jnp.arange(n, dtype=jnp.int32)[:, None]
    out_row = ((tok // COMBINE_TILE) * (TOP_K * COMBINE_TILE) + jnp.arange(TOP_K, dtype=jnp.int32)[None, :] * COMBINE_TILE
               + tok % COMBINE_TILE)
    counts = counts[0, :N_EXPERTS].astype(jnp.int32)
    padded = ((counts + tm - 1) // tm) * tm
    pad_c = jnp.arange(tm, dtype=jnp.int32)
    pad_expert = jnp.where(pad_c[None, :] < (padded - counts)[:, None],
                           jnp.arange(N_EXPERTS, dtype=jnp.int32)[:, None], N_EXPERTS)
    keys = jnp.concatenate([(idx * (1 << id_bits) + out_row).reshape(-1),
                            (pad_expert * (1 << id_bits) + (n_ids + pad_c)[None, :]).reshape(-1)])
    rows = jnp.sort(keys) & ((1 << id_bits) - 1)
    rows = jnp.concatenate([n_ids + pad_c, rows, n_ids + pad_c, n_ids + pad_c])
    blk = TOP_K * COMBINE_TILE
    src = jnp.minimum((rows // blk) * COMBINE_TILE + rows % COMBINE_TILE, n - 1)
    a_ext = lax.bitcast_convert_type((src.astype(jnp.uint32) << ID_BITS) | rows.astype(jnp.uint32), jnp.int32)
    ends = jnp.cumsum(padded)
    max_tiles = n_ids // tm + N_EXPERTS
    n_tiles = ends[-1] // tm
    tile_rows = jnp.minimum(jnp.arange(max_tiles + 2, dtype=jnp.int32), n_tiles - 1) * tm
    tile_expert = jnp.minimum(jnp.sum(ends[None, :] <= tile_rows[:, None], axis=1), N_EXPERTS - 1).astype(jnp.int32)
    ys = _experts(a_ext, tile_expert, n_tiles.reshape(1).astype(jnp.int32), h_tiles, layer, w_up, b_up, w_down, b_down)
    return _combine(x, gates, g2, g_final, ys, cond_of_tile, final)


def _rope_tables(n_prompt, batch, length):
    rows = jnp.arange(length) // GRID_W
    cols = jnp.arange(length) % GRID_W
    nfreq = HEAD_DIM // 4
    inv = ROPE_BASE ** (-jnp.arange(nfreq, dtype=F32) / nfreq)
    lane = jnp.arange(LANES)
    is_col = (lane % HEAD_DIM) >= HEAD_DIM // 2
    pos = jnp.where(is_col[None, :], cols[:, None], rows[:, None]).astype(F32)
    ang = pos * inv[lane % nfreq][None, :]
    sign = jnp.where((lane % (2 * nfreq)) < nfreq, -1.0, 1.0)[None, :]
    cos = jnp.tile(jnp.cos(ang), (batch, 1))
    sin = jnp.tile(jnp.sin(ang) * sign, (batch, 1))
    cos = jnp.concatenate([jnp.ones((n_prompt, LANES), F32), cos], axis=0)
    sin = jnp.concatenate([jnp.zeros((n_prompt, LANES), F32), sin], axis=0)
    return cos, sin


def _block_diag_pairs(w):
    w = w.reshape(2, 4, 2, RNN_BLOCK, RNN_BLOCK)
    z = jnp.zeros_like(w[:, :, 0])
    top = jnp.concatenate([w[:, :, 0], z], axis=-1)
    bot = jnp.concatenate([z, w[:, :, 1]], axis=-1)
    return jnp.concatenate([top, bot], axis=-2)


def kernel(x_prompt, x_sample, c, cache_k, cache_v, state_h, c_ctx, w_ada, b_ada, g_norm1, g_norm2, g_final, w_in_ab, sink, w_spatial, b_spatial, g_sgu, w_in_cd, conv_c_w, conv_c_b, w_rg_a, b_rg_a, w_rg_i, b_rg_i, lam, conv_d_w, conv_d_b, ln_d_g, ln_d_b, w_out, w_router, b_router, w_up, b_up, w_down, b_down):
    batch, seq, d = x_prompt.shape
    dec_batch, dec_seq, _ = x_sample.shape
    depth = w_ada.shape[0]
    n_p = batch * seq
    n_s = dec_batch * dec_seq
    past = cache_k.shape[2]
    assert n_p % ROW_TILE == 0 and dec_seq % ROW_TILE == 0 and dec_batch + 1 <= 8
    prompt_tiles = n_p // ROW_TILE
    tiles_per_seq = dec_seq // ROW_TILE

    def cond_of_tile(i):
        return jnp.where(i < prompt_tiles, 0, 1 + (i - prompt_tiles) // tiles_per_seq)

    x = jnp.concatenate([x_prompt.reshape(n_p, d), x_sample.reshape(n_s, d)], axis=0)
    cond8 = jnp.zeros((8, d), F32).at[0].set(c_ctx).at[1:1 + dec_batch].set(c)
    mods = _adaln(cond8, w_ada, b_ada)
    mods = mods.reshape(depth, 8, N_MOD, d).transpose(0, 2, 1, 3)[:, :, :, None, :]
    tables = _rope_tables(n_p, dec_batch, dec_seq)

    new_k, new_v, new_h = [], [], []
    for l in range(depth):
        j = l // 2
        sh1, sc1, g1, sh2, sc2, g2 = (mods[l, m] for m in range(N_MOD))
        if l % 2 == 0:
            w = w_in_ab[j]
            q_end, k_end, v_end = ATT_WIDTH, ATT_WIDTH + KV_WIDTH, ATT_WIDTH + 2 * KV_WIDTH
            w = jnp.concatenate([w[:, :q_end], w[:, v_end:], w[:, q_end:v_end]], axis=1).astype(BF16)
            proj, kv = _inproj(x, g_norm1[l], sc1, sh1, w, cond_of_tile, tables)
            new_k.append(kv[:n_p, :KV_WIDTH].reshape(batch, seq, N_KV_HEADS, HEAD_DIM))
            new_v.append(kv[:n_p, KV_WIDTH:].reshape(batch, seq, N_KV_HEADS, HEAD_DIM))
            sink_b = jnp.broadcast_to(sink[j][:, None], (N_Q_HEADS, LANES))
            att_p = _ctx_attention(proj, sink_b, batch, seq)
            ctx_k = cache_k[:, j].reshape(dec_batch, past, KV_WIDTH).astype(BF16)
            ctx_v = cache_v[:, j].reshape(dec_batch, past, KV_WIDTH).astype(BF16)
            att_s = _lat_attention(proj, sink_b, ctx_k, ctx_v, n_p, dec_batch, dec_seq)
            mix_a = jnp.concatenate([att_p, att_s], axis=0)
            mix_b = _sgu(proj, g_sgu[j], w_spatial[j].astype(BF16), b_spatial[j].T)
        else:
            proj = _inproj(x, g_norm1[l], sc1, sh1, w_in_cd[j].astype(BF16), cond_of_tile)
            wa = _block_diag_pairs(w_rg_a[j])
            wi = _block_diag_pairs(w_rg_i[j])
            w4 = jnp.concatenate([wa[0], wi[0], wa[1], wi[1]], axis=-1).astype(BF16)
            strips = lambda v: v.reshape(RNN_WIDTH // LANES, 1, LANES)
            b4 = jnp.concatenate([strips(b_rg_a[j, 0]), strips(b_rg_i[j, 0]),
                                  strips(b_rg_a[j, 1]), strips(b_rg_i[j, 1])], axis=-1)
            h0_p = jnp.zeros((batch, 2, RNN_WIDTH), F32)
            c_p, h_fin = _rglru(proj, 0, batch, seq, conv_c_w[j], conv_c_b[j], w4, b4, lam[j], h0_p)
            c_s, _ = _rglru(proj, n_p, dec_batch, dec_seq, conv_c_w[j], conv_c_b[j], w4, b4, lam[j], state_h[:, j])
            new_h.append(h_fin)
            w_pad = jnp.concatenate([conv_d_w[j], jnp.zeros((4 * SUBLANES - CONV_K, CONV_WIDTH), F32)], axis=0)
            d_p = _convmod(proj, 0, batch, seq, w_pad, conv_d_b[j], ln_d_g[j], ln_d_b[j])
            d_s = _convmod(proj, n_p, dec_batch, dec_seq, w_pad, conv_d_b[j], ln_d_g[j], ln_d_b[j])
            mix_a = jnp.concatenate([c_p, c_s], axis=0)
            mix_b = jnp.concatenate([d_p, d_s], axis=0)
        w_r = jnp.concatenate([w_router[l], jnp.zeros((d, LANES - N_EXPERTS), F32)], axis=1)
        w_r_hi = w_r.astype(BF16)
        w_r = jnp.concatenate([w_r_hi, (w_r - w_r_hi.astype(F32)).astype(BF16)], axis=1)
        b_r = jnp.concatenate([b_router[l], jnp.full((LANES - N_EXPERTS,), NEG_INF, F32)]).reshape(1, LANES)
        x, h_tiles, logits = _outproj(x, mix_a, mix_b, w_out[l].astype(BF16), g1, g_norm2[l], sc2, sh2, w_r, b_r, cond_of_tile)
        x = _moe(x, h_tiles, logits, g2, g_final, l, w_up, b_up, w_down, b_down, cond_of_tile, final=(l == depth - 1))

    y_prompt = x[:n_p].reshape(batch, seq, d)
    y_sample = x[n_p:].reshape(dec_batch, dec_seq, d)
    return (y_prompt, y_sample, jnp.stack(new_k, axis=1), jnp.stack(new_v, axis=1), jnp.stack(new_h, axis=1))
```
